```python
import math
import jax, jax.numpy as jnp
from jax import lax
import numpy as np

D_MODEL = 1024
BATCH = 8
SEQ = 8192
DEPTH = 2
DEC_BATCH = 4
DEC_SEQ = 4096
PAST_LEN = 128

EPS = 1e-6
ROPE_THETA = 10000.0
N_BRANCH = 3
CHUNK = 128
A_GROUPS = 8
A_WIDTH = 1024
A_GD = A_WIDTH // A_GROUPS
B_HEADS = 8
B_Q_RANK = 384
B_KV_RANK = 256
B_NOPE = 128
B_ROPE = 64
B_VDIM = 128
B_WIDTH = B_HEADS * B_VDIM
Q_BLOCK = 128
C_HEADS = 8
C_KV_HEADS = 2
C_HD = 128
C_WIDTH = C_HEADS * C_HD
C_KV_WIDTH = C_KV_HEADS * C_HD
WINDOW = 128
C_BLOCK = 128
D_FF = -(-8 * D_MODEL // (3 * 256)) * 256
IN_SIZES = (A_WIDTH, A_WIDTH, B_Q_RANK, B_KV_RANK, B_ROPE, C_WIDTH, C_KV_WIDTH, C_KV_WIDTH, N_BRANCH * D_MODEL)
IN_COLS = A_WIDTH * 2 + B_Q_RANK + B_KV_RANK + B_ROPE + C_WIDTH + 2 * C_KV_WIDTH + N_BRANCH * D_MODEL

kernel_name = "hybrid_gated_gmlp_mla_swa_encoder"


def rmsnorm(x, g):
    xf = x.astype(jnp.float32)
    y = xf * lax.rsqrt(jnp.mean(xf * xf, axis=-1, keepdims=True) + EPS)
    return (y * g.astype(jnp.float32)).astype(x.dtype)


def layernorm(x, g, b):
    xf = x.astype(jnp.float32)
    mu = jnp.mean(xf, axis=-1, keepdims=True)
    xc = xf - mu
    var = jnp.mean(xc * xc, axis=-1, keepdims=True)
    y = xc * lax.rsqrt(var + EPS) * g.astype(jnp.float32) + b.astype(jnp.float32)
    return y.astype(x.dtype)


def rope_tables(seq, dim):
    inv = 1.0 / (ROPE_THETA ** (jnp.arange(0, dim, 2, dtype=jnp.float32) / dim))
    ang = jnp.arange(seq, dtype=jnp.float32)[:, None] * inv[None, :]
    return jnp.cos(ang), jnp.sin(ang)


def apply_rope(x, cos, sin):
    xf = x.astype(jnp.float32)
    half = x.shape[-1] // 2
    x1, x2 = xf[..., :half], xf[..., half:]
    c = cos[None, :, None, :]
    s = sin[None, :, None, :]
    return jnp.concatenate([x1 * c - x2 * s, x2 * c + x1 * s], axis=-1).astype(x.dtype)


def spatial_gating(u, v, ln_g, ln_b, w_s, b_s):
    bsz, seq, _ = v.shape
    u = jax.nn.gelu(u)
    v = layernorm(jax.nn.gelu(v), ln_g, ln_b)
    vc = v.reshape(bsz, seq // CHUNK, CHUNK, A_GROUPS, A_GD)
    mixed = jnp.einsum('gij,bcjgd->bcigd', w_s, vc) + b_s.T[None, None, :, :, None]
    return u * mixed.reshape(bsz, seq, A_WIDTH)


def mla(c_q, c_kv, k_rope_in, q_norm_g, w_uq, kv_norm_g, w_ukv, cos, sin):
    bsz, seq, _ = c_q.shape
    q = (rmsnorm(c_q, q_norm_g) @ w_uq).reshape(bsz, seq, B_HEADS, B_NOPE + B_ROPE)
    q_nope = q[..., :B_NOPE]
    q_pe = apply_rope(q[..., B_NOPE:], cos, sin)
    kv = (rmsnorm(c_kv, kv_norm_g) @ w_ukv).reshape(bsz, seq, B_HEADS, B_NOPE + B_VDIM)
    k_nope = kv[..., :B_NOPE]
    v = kv[..., B_NOPE:]
    k_pe = apply_rope(k_rope_in[:, :, None, :], cos, sin)[:, :, 0, :]
    scale = (B_NOPE + B_ROPE) ** -0.5
    nqb = seq // Q_BLOCK
    qn_blk = q_nope.reshape(bsz, nqb, Q_BLOCK, B_HEADS, B_NOPE).transpose(1, 0, 2, 3, 4)
    qp_blk = q_pe.reshape(bsz, nqb, Q_BLOCK, B_HEADS, B_ROPE).transpose(1, 0, 2, 3, 4)

    def attend(blk):
        qn, qp = blk
        s = (jnp.einsum('bqhd,bkhd->bhqk', qn, k_nope) + jnp.einsum('bqhr,bkr->bhqk', qp, k_pe)).astype(jnp.float32) * scale
        p = jax.nn.softmax(s, axis=-1).astype(v.dtype)
        return jnp.einsum('bhqk,bkhd->bqhd', p, v)

    o = lax.map(attend, (qn_blk, qp_blk))
    return o.transpose(1, 0, 2, 3, 4).reshape(bsz, seq, B_WIDTH)


def window_gqa(q, k, v, sink, cos, sin):
    bsz, seq, _ = q.shape
    rep = C_HEADS // C_KV_HEADS
    nb = seq // C_BLOCK
    q = apply_rope(q.reshape(bsz, seq, C_HEADS, C_HD), cos, sin)
    k = apply_rope(k.reshape(bsz, seq, C_KV_HEADS, C_HD), cos, sin)
    v = v.reshape(bsz, seq, C_KV_HEADS, C_HD)
    qb = q.reshape(bsz, nb, C_BLOCK, C_KV_HEADS, rep, C_HD)
    pad = ((0, 0), (C_BLOCK, C_BLOCK), (0, 0), (0, 0))
    kp = jnp.pad(k, pad).reshape(bsz, nb + 2, C_BLOCK, C_KV_HEADS, C_HD)
    vp = jnp.pad(v, pad).reshape(bsz, nb + 2, C_BLOCK, C_KV_HEADS, C_HD)
    kb = jnp.concatenate([kp[:, :-2], kp[:, 1:-1], kp[:, 2:]], axis=2)
    vb = jnp.concatenate([vp[:, :-2], vp[:, 1:-1], vp[:, 2:]], axis=2)
    s = jnp.einsum('bnqgrd,bnkgd->bngrqk', qb, kb).astype(jnp.float32) * (C_HD ** -0.5)
    qi = jnp.arange(C_BLOCK)[:, None]
    kj = jnp.arange(3 * C_BLOCK)[None, :] - C_BLOCK
    kabs = (jnp.arange(nb) * C_BLOCK)[:, None, None] + kj[None]
    valid = (jnp.abs(kj - qi) <= WINDOW)[None] & (kabs >= 0) & (kabs < seq)
    s = jnp.where(valid[None, :, None, None, :, :], s, -jnp.inf)
    sink_b = sink.astype(jnp.float32).reshape(1, 1, C_KV_HEADS, rep, 1, 1)
    m = jnp.maximum(jnp.max(s, axis=-1, keepdims=True), sink_b)
    e = jnp.exp(s - m)
    p = e / (jnp.sum(e, axis=-1, keepdims=True) + jnp.exp(sink_b - m))
    o = jnp.einsum('bngrqk,bnkgd->bnqgrd', p.astype(v.dtype), vb)
    return o.reshape(bsz, seq, C_WIDTH)


def encoder_layer(x, pre_mix_g, w_in, a_ln_g, a_ln_b, a_w_s, a_b_s, b_q_norm_g, b_w_uq, b_kv_norm_g, b_w_ukv,
                  c_sink, w_pa, w_pb, w_pc, w_o, post_mix_g, pre_ffn_g, w_gate, w_up, w_down, post_ffn_g,
                  rope_b, rope_c):
    bsz, seq, _ = x.shape
    h = rmsnorm(x, pre_mix_g)
    proj = h @ w_in
    idx = np.cumsum(IN_SIZES)[:-1].tolist()
    u, v, c_q, c_kv, k_r, cq, ck, cv, gates = jnp.split(proj, idx, axis=-1)
    y_a = spatial_gating(u, v, a_ln_g, a_ln_b, a_w_s, a_b_s)
    y_b = mla(c_q, c_kv, k_r, b_q_norm_g, b_w_uq, b_kv_norm_g, b_w_ukv, rope_b[0], rope_b[1])
    y_c = window_gqa(cq, ck, cv, c_sink, rope_c[0], rope_c[1])
    g = jax.nn.sigmoid(gates.astype(jnp.float32)).astype(x.dtype).reshape(bsz, seq, N_BRANCH, D_MODEL)
    merged = g[:, :, 0] * (y_a @ w_pa) + g[:, :, 1] * (y_b @ w_pb) + g[:, :, 2] * (y_c @ w_pc)
    x = x + rmsnorm(merged @ w_o, post_mix_g)
    h = rmsnorm(x, pre_ffn_g)
    f = (jax.nn.silu(h @ w_gate) * (h @ w_up)) @ w_down
    return x + rmsnorm(f, post_ffn_g)


def trunk(x, pre_mix_g, w_in, a_ln_g, a_ln_b, a_w_s, a_b_s, b_q_norm_g, b_w_uq, b_kv_norm_g, b_w_ukv,
          c_sink, w_pa, w_pb, w_pc, w_o, post_mix_g, pre_ffn_g, w_gate, w_up, w_down, post_ffn_g):
    seq = x.shape[1]
    rope_b = rope_tables(seq, B_ROPE)
    rope_c = rope_tables(seq, C_HD)
    for l in range(DEPTH):
        x = encoder_layer(x, pre_mix_g[l], w_in[l], a_ln_g[l], a_ln_b[l], a_w_s[l], a_b_s[l],
                          b_q_norm_g[l], b_w_uq[l], b_kv_norm_g[l], b_w_ukv[l], c_sink[l],
                          w_pa[l], w_pb[l], w_pc[l], w_o[l], post_mix_g[l], pre_ffn_g[l],
                          w_gate[l], w_up[l], w_down[l], post_ffn_g[l], rope_b, rope_c)
    return x


def setup_inputs(seed: int = 0) -> dict:
    key = jax.random.key(seed)
    ks = jax.random.split(key, 24)

    def nrm(k, shape, scale):
        return jax.random.normal(k, shape, jnp.float32) * scale

    def gain(k, n):
        return 1.0 + nrm(k, (DEPTH, n), 0.02)

    return {
        "x_prompt": nrm(ks[0], (BATCH, SEQ, D_MODEL), 1.0),
        "x_sample": nrm(ks[1], (DEC_BATCH, DEC_SEQ, D_MODEL), 1.0),
        "pre_mix_g": gain(ks[2], D_MODEL),
        "w_in": nrm(ks[3], (DEPTH, D_MODEL, IN_COLS), D_MODEL ** -0.5),
        "a_ln_g": gain(ks[4], A_WIDTH),
        "a_ln_b": nrm(ks[5], (DEPTH, A_WIDTH), 0.02),
        "a_w_s": nrm(ks[6], (DEPTH, A_GROUPS, CHUNK, CHUNK), CHUNK ** -0.5),
        "a_b_s": 1.0 + nrm(ks[7], (DEPTH, A_GROUPS, CHUNK), 0.02),
        "b_q_norm_g": gain(ks[8], B_Q_RANK),
        "b_w_uq": nrm(ks[9], (DEPTH, B_Q_RANK, B_HEADS * (B_NOPE + B_ROPE)), B_Q_RANK ** -0.5),
        "b_kv_norm_g": gain(ks[10], B_KV_RANK),
        "b_w_ukv": nrm(ks[11], (DEPTH, B_KV_RANK, B_HEADS * (B_NOPE + B_VDIM)), B_KV_RANK ** -0.5),
        "c_sink": nrm(ks[12], (DEPTH, C_HEADS), 0.5),
        "w_pa": nrm(ks[13], (DEPTH, A_WIDTH, D_MODEL), A_WIDTH ** -0.5),
        "w_pb": nrm(ks[14], (DEPTH, B_WIDTH, D_MODEL), B_WIDTH ** -0.5),
        "w_pc": nrm(ks[15], (DEPTH, C_WIDTH, D_MODEL), C_WIDTH ** -0.5),
        "w_o": nrm(ks[16], (DEPTH, D_MODEL, D_MODEL), D_MODEL ** -0.5),
        "post_mix_g": gain(ks[17], D_MODEL),
        "pre_ffn_g": gain(ks[18], D_MODEL),
        "w_gate": nrm(ks[19], (DEPTH, D_MODEL, D_FF), D_MODEL ** -0.5),
        "w_up": nrm(ks[20], (DEPTH, D_MODEL, D_FF), D_MODEL ** -0.5),
        "w_down": nrm(ks[21], (DEPTH, D_FF, D_MODEL), D_FF ** -0.5),
        "post_ffn_g": gain(ks[22], D_MODEL),
    }


def reference(x_prompt, x_sample, pre_mix_g, w_in, a_ln_g, a_ln_b, a_w_s, a_b_s, b_q_norm_g, b_w_uq,
              b_kv_norm_g, b_w_ukv, c_sink, w_pa, w_pb, w_pc, w_o, post_mix_g, pre_ffn_g, w_gate, w_up,
              w_down, post_ffn_g):
    y_prompt = trunk(x_prompt, pre_mix_g, w_in, a_ln_g, a_ln_b, a_w_s, a_b_s, b_q_norm_g, b_w_uq,
                     b_kv_norm_g, b_w_ukv, c_sink, w_pa, w_pb, w_pc, w_o, post_mix_g, pre_ffn_g,
                     w_gate, w_up, w_down, post_ffn_g)
    y_sample = trunk(x_sample, pre_mix_g, w_in, a_ln_g, a_ln_b, a_w_s, a_b_s, b_q_norm_g, b_w_uq,
                     b_kv_norm_g, b_w_ukv, c_sink, w_pa, w_pb, w_pc, w_o, post_mix_g, pre_ffn_g,
                     w_gate, w_up, w_down, post_ffn_g)
    return (y_prompt, y_sample)
```

```python
import functools
import math

import jax
import jax.numpy as jnp
from jax import lax
from jax.experimental import pallas as pl
from jax.experimental.pallas import tpu as pltpu

F32 = jnp.float32
BF16 = jnp.bfloat16

D_MODEL = 1024
EPS = 1e-6
ROPE_THETA = 10000.0
N_BRANCH = 3
CHUNK = 128
A_GROUPS = 8
A_WIDTH = 1024
B_HEADS = 8
B_Q_RANK = 384
B_KV_RANK = 256
B_NOPE = 128
B_ROPE = 64
B_VDIM = 128
C_HEADS = 8
C_KV_HEADS = 2
C_HD = 128
C_WIDTH = C_HEADS * C_HD
C_KV_WIDTH = C_KV_HEADS * C_HD
C_REP = C_HEADS // C_KV_HEADS
WINDOW = 128
D_FF = -(-8 * D_MODEL // (3 * 256)) * 256
IN_SIZES = (A_WIDTH, A_WIDTH, B_Q_RANK, B_KV_RANK, B_ROPE, C_WIDTH, C_KV_WIDTH, C_KV_WIDTH, N_BRANCH * D_MODEL)

LANES = 128
B_QK_PAD = 2 * LANES
LOG2E = math.log2(math.e)
B_QSCALE = (B_NOPE + B_ROPE) ** -0.5 * LOG2E
C_QSCALE = C_HD ** -0.5 * LOG2E

V7X_VMEM_BYTES = 64 * 1024 * 1024
VMEM_LIMIT = V7X_VMEM_BYTES - 8 * 1024 * 1024

TM_PROJ = 256
TM_MERGE = 256
TM_FFN = 256
TQ_MLA = 512
TK_MLA = 512
SWA_BLOCKS = 4


def _dot(a, b):
    return jnp.dot(a, b, preferred_element_type=F32)


def _dot_nt(a, b):
    return lax.dot_general(a, b, (((1,), (1,)), ((), ())), preferred_element_type=F32)


def _rms(x, g):
    return x * lax.rsqrt(jnp.mean(x * x, axis=-1, keepdims=True) + EPS) * g


def _const_spec(arr):
    nd = arr.ndim
    return pl.BlockSpec(arr.shape, lambda *_: (0,) * nd, pipeline_mode=pl.Buffered(1))


def _params(n_axes):
    return pltpu.CompilerParams(dimension_semantics=("arbitrary",) * n_axes, vmem_limit_bytes=VMEM_LIMIT)


def _proj_kernel(x_ref, g_ref, wuv_ref, wlat_ref, wc_ref, lng_ref, lnb_ref, qg_ref, kvg_ref,
                 wuq_ref, wuqs_ref, wuk_ref, wuvv_ref, cb_ref, sb_ref, cc_ref, sc_ref,
                 h_ref, u_ref, v_ref, qm_ref, km_ref, vm_ref, cq_ref, ck_ref, cv_ref):
    h = _rms(x_ref[0], g_ref[...]).astype(BF16)
    h_ref[0] = h

    nc = 2 * LANES
    for c in range(0, A_WIDTH, nc):
        u_ref[0, :, c:c + nc] = jax.nn.gelu(_dot(h, wuv_ref[:, c:c + nc])).astype(BF16)
    v = jax.nn.gelu(_dot(h, wuv_ref[:, A_WIDTH:]))
    vc = v - jnp.mean(v, axis=-1, keepdims=True)
    var = jnp.mean(vc * vc, axis=-1, keepdims=True)
    v_ref[0] = (vc * lax.rsqrt(var + EPS) * lng_ref[...] + lnb_ref[...]).astype(BF16)

    lat = _dot(h, wlat_ref[...])
    o1 = B_Q_RANK
    o2 = o1 + B_KV_RANK
    cqn = _rms(lat[:, :o1], qg_ref[...]).astype(BF16)
    ckvn = _rms(lat[:, o1:o2], kvg_ref[...]).astype(BF16)
    cb = cb_ref[...]
    sb = sb_ref[...]
    kpe = (lat[:, o2:o2 + LANES] * cb + lat[:, o2 + LANES:o2 + 2 * LANES] * sb).astype(BF16)
    for hd in range(B_HEADS):
        c0 = hd * B_QK_PAD
        qh = _dot(cqn, wuq_ref[:, c0:c0 + B_QK_PAD])
        qs = _dot(cqn, wuqs_ref[:, hd * LANES:(hd + 1) * LANES])
        qm_ref[0, :, c0:c0 + LANES] = (qh[:, :LANES] * B_QSCALE).astype(BF16)
        qm_ref[0, :, c0 + LANES:c0 + B_QK_PAD] = ((qh[:, LANES:] * cb + qs * sb) * B_QSCALE).astype(BF16)
        km_ref[0, :, c0 + LANES:c0 + B_QK_PAD] = kpe
    for hp in range(0, B_HEADS, 2):
        kn = _dot(ckvn, wuk_ref[:, hp * LANES:(hp + 2) * LANES])
        km_ref[0, :, hp * B_QK_PAD:hp * B_QK_PAD + LANES] = kn[:, :LANES].astype(BF16)
        km_ref[0, :, (hp + 1) * B_QK_PAD:(hp + 1) * B_QK_PAD + LANES] = kn[:, LANES:].astype(BF16)
        vm_ref[0, :, hp * LANES:(hp + 2) * LANES] = _dot(ckvn, wuvv_ref[:, hp * LANES:(hp + 2) * LANES]).astype(BF16)

    cc = cc_ref[...]
    sc = sc_ref[...]
    for c in range(0, C_WIDTH + C_KV_WIDTH, nc):
        qk = _dot(h, wc_ref[:, c:c + nc])
        for j in range(nc // LANES):
            t = qk[:, j * LANES:(j + 1) * LANES]
            r = t * cc + pltpu.roll(t, C_HD // 2, 1) * sc
            col = c + j * LANES
            if col < C_WIDTH:
                cq_ref[0, :, col:col + LANES] = (r * C_QSCALE).astype(BF16)
            else:
                ck_ref[0, :, col - C_WIDTH:col - C_WIDTH + LANES] = r.astype(BF16)
    cv_ref[0] = _dot(h, wc_ref[:, C_WIDTH + C_KV_WIDTH:]).astype(BF16)


def _proj(x, lw, rope):
    bsz, seq, _ = x.shape
    tm = min(TM_PROJ, seq)
    grid = (bsz, seq // tm)
    tok = lambda w: pl.BlockSpec((1, tm, w), lambda b, i: (b, i, 0))
    pos = pl.BlockSpec((tm, LANES), lambda b, i: (i, 0))
    consts = [lw["pre_mix_g"], lw["w_uv"], lw["w_lat"], lw["w_c"], lw["a_ln_g"], lw["a_ln_b"], lw["b_q_norm_g"],
              lw["b_kv_norm_g"], lw["w_uq"], lw["w_uq_sw"], lw["w_uk"], lw["w_uvv"]]
    widths = [D_MODEL, A_WIDTH, A_WIDTH, B_HEADS * B_QK_PAD, B_HEADS * B_QK_PAD, B_HEADS * B_VDIM,
              C_WIDTH, C_KV_WIDTH, C_KV_WIDTH]
    return pl.pallas_call(
        _proj_kernel,
        grid=grid,
        in_specs=[tok(D_MODEL)] + [_const_spec(a) for a in consts] + [pos] * 4,
        out_specs=[tok(w) for w in widths],
        out_shape=[jax.ShapeDtypeStruct((bsz, seq, w), BF16) for w in widths],
        compiler_params=_params(2),
        name="proj",
    )(x, *consts, *rope)


def _mla_kernel(q_ref, k_ref, v_ref, o_ref, m_sc, l_sc, acc_sc, *, tk):
    seq = k_ref.shape[1]
    q = q_ref[0]
    m_sc[...] = jnp.full(m_sc.shape, -jnp.inf, F32)
    l_sc[...] = jnp.zeros(l_sc.shape, F32)
    acc_sc[...] = jnp.zeros(acc_sc.shape, F32)

    def body(j, carry):
        off = pl.multiple_of(j * tk, tk)
        s = _dot_nt(q, k_ref[0, pl.ds(off, tk), :])
        chunks = [s[:, c * LANES:(c + 1) * LANES] for c in range(tk // LANES)]
        m_prev = m_sc[...]
        m_cur = functools.reduce(jnp.maximum, chunks)
        m_new = jnp.maximum(m_prev, jnp.max(m_cur, axis=-1, keepdims=True))
        alpha = jnp.exp2(m_prev - m_new)
        ps = [jnp.exp2(c - m_new) for c in chunks]
        l_sc[...] = alpha * l_sc[...] + functools.reduce(jnp.add, ps)
        p = jnp.concatenate([c.astype(BF16) for c in ps], axis=1)
        acc_sc[...] = alpha * acc_sc[...] + _dot(p, v_ref[0, pl.ds(off, tk), :])
        m_sc[...] = m_new
        return carry

    lax.fori_loop(0, seq // tk, body, 0)
    o_ref[0] = (acc_sc[...] / jnp.sum(l_sc[...], axis=-1, keepdims=True)).astype(BF16)


def _mla(qm, km, vm):
    bsz, seq, _ = qm.shape
    tq = min(TQ_MLA, seq)
    tk = min(TK_MLA, seq)
    return pl.pallas_call(
        functools.partial(_mla_kernel, tk=tk),
        grid=(bsz, B_HEADS, seq // tq),
        in_specs=[
            pl.BlockSpec((1, tq, B_QK_PAD), lambda b, h, i: (b, i, h)),
            pl.BlockSpec((1, seq, B_QK_PAD), lambda b, h, i: (b, 0, h)),
            pl.BlockSpec((1, seq, B_VDIM), lambda b, h, i: (b, 0, h)),
        ],
        out_specs=pl.BlockSpec((1, tq, B_VDIM), lambda b, h, i: (b, i, h)),
        out_shape=jax.ShapeDtypeStruct((bsz, seq, B_HEADS * B_VDIM), BF16),
        scratch_shapes=[pltpu.VMEM((tq, LANES), F32), pltpu.VMEM((tq, LANES), F32), pltpu.VMEM((tq, B_VDIM), F32)],
        compiler_params=_params(3),
        name="mla",
    )(qm, km, vm)


def _swa_kernel(q_ref, kc_ref, kp_ref, kn_ref, vc_ref, vp_ref, vn_ref, sink_ref, o_ref, *, nblk):
    i = pl.program_id(1)
    last = pl.num_programs(1) - 1
    rows = C_REP * CHUNK
    qi = lax.broadcasted_iota(jnp.int32, (rows, CHUNK), 0) % CHUNK
    kj = lax.broadcasted_iota(jnp.int32, (rows, CHUNK), 1)
    band_prev = kj >= qi
    band_next = kj <= qi
    neg = jnp.float32(-jnp.inf)

    def blk(ref_c, ref_e, n, g, edge_lo):
        cols = slice(g * C_HD, (g + 1) * C_HD)
        if edge_lo:
            return ref_e[0, :, cols] if n == 0 else ref_c[0, (n - 1) * CHUNK:n * CHUNK, cols]
        return ref_e[0, :, cols] if n == nblk - 1 else ref_c[0, (n + 1) * CHUNK:(n + 2) * CHUNK, cols]

    for n in range(nblk):
        tok = slice(n * CHUNK, (n + 1) * CHUNK)
        for g in range(C_KV_HEADS):
            cols = slice(g * C_HD, (g + 1) * C_HD)
            q4 = jnp.concatenate([q_ref[0, tok, (g * C_REP + r) * C_HD:(g * C_REP + r + 1) * C_HD]
                                  for r in range(C_REP)], axis=0)
            sink = jnp.concatenate([jnp.broadcast_to(sink_ref[g * C_REP + r:g * C_REP + r + 1, :], (CHUNK, LANES))
                                    for r in range(C_REP)], axis=0)
            k3 = jnp.concatenate([blk(kc_ref, kp_ref, n, g, True), kc_ref[0, tok, cols],
                                  blk(kc_ref, kn_ref, n, g, False)], axis=0)
            v3 = jnp.concatenate([blk(vc_ref, vp_ref, n, g, True), vc_ref[0, tok, cols],
                                  blk(vc_ref, vn_ref, n, g, False)], axis=0)
            s = _dot_nt(q4, k3)
            ok_prev = band_prev if n > 0 else jnp.logical_and(band_prev, i > 0)
            ok_next = band_next if n < nblk - 1 else jnp.logical_and(band_next, i < last)
            s0 = jnp.where(ok_prev, s[:, :CHUNK], neg)
            s1 = s[:, CHUNK:2 * CHUNK]
            s2 = jnp.where(ok_next, s[:, 2 * CHUNK:], neg)
            m = jnp.max(jnp.maximum(jnp.maximum(s0, s1), s2), axis=-1, keepdims=True)
            m = jnp.maximum(m, sink)
            es = [jnp.exp2(t - m) for t in (s0, s1, s2)]
            den = jnp.sum(es[0] + es[1] + es[2], axis=-1, keepdims=True) + jnp.exp2(sink - m)
            p = jnp.concatenate([e.astype(BF16) for e in es], axis=1)
            o = _dot(p, v3) / den
            for r in range(C_REP):
                hc = (g * C_REP + r) * C_HD
                o_ref[0, tok, hc:hc + C_HD] = o[r * CHUNK:(r + 1) * CHUNK].astype(BF16)


def _swa(cq, ck, cv, sink):
    bsz, seq, _ = cq.shape
    nblk = min(SWA_BLOCKS, seq // CHUNK)
    ts = nblk * CHUNK
    nb = seq // CHUNK
    cur = lambda w: pl.BlockSpec((1, ts, w), lambda b, i: (b, i, 0))
    prev = pl.BlockSpec((1, CHUNK, C_KV_WIDTH), lambda b, i: (b, jnp.maximum(i * nblk - 1, 0), 0))
    nxt = pl.BlockSpec((1, CHUNK, C_KV_WIDTH), lambda b, i: (b, jnp.minimum((i + 1) * nblk, nb - 1), 0))
    return pl.pallas_call(
        functools.partial(_swa_kernel, nblk=nblk),
        grid=(bsz, seq // ts),
        in_specs=[cur(C_WIDTH), cur(C_KV_WIDTH), prev, nxt, cur(C_KV_WIDTH), prev, nxt, _const_spec(sink)],
        out_specs=cur(C_WIDTH),
        out_shape=jax.ShapeDtypeStruct((bsz, seq, C_WIDTH), BF16),
        compiler_params=_params(2),
        name="swa",
    )(cq, ck, ck, ck, cv, cv, cv, sink)


def _merge_kernel(x_ref, h_ref, u_ref, v_ref, ob_ref, oc_ref, wg_ref, ws_ref, bs_ref, wpa_ref, wpb_ref, wpc_ref,
                  wo_ref, g_ref, out_ref, ya_sc, mg_sc):
    tm = x_ref.shape[1]
    for c in range(tm // CHUNK):
        tok = slice(c * CHUNK, (c + 1) * CHUNK)
        for g in range(A_GROUPS):
            cols = slice(g * LANES, (g + 1) * LANES)
            mixed = _dot(ws_ref[g], v_ref[0, tok, cols]) + bs_ref[g]
            ya_sc[tok, cols] = (u_ref[0, tok, cols].astype(F32) * mixed).astype(BF16)
    h = h_ref[0]
    ya = ya_sc[...]
    ob = ob_ref[0]
    oc = oc_ref[0]
    nc = 2 * LANES
    for c in range(0, D_MODEL, nc):
        cols = slice(c, c + nc)
        acc = jax.nn.sigmoid(_dot(h, wg_ref[:, c:c + nc])) * _dot(ya, wpa_ref[:, cols])
        acc += jax.nn.sigmoid(_dot(h, wg_ref[:, D_MODEL + c:D_MODEL + c + nc])) * _dot(ob, wpb_ref[:, cols])
        acc += jax.nn.sigmoid(_dot(h, wg_ref[:, 2 * D_MODEL + c:2 * D_MODEL + c + nc])) * _dot(oc, wpc_ref[:, cols])
        mg_sc[:, cols] = acc.astype(BF16)
    mix = _dot(mg_sc[...], wo_ref[...])
    out_ref[0] = x_ref[0] + _rms(mix, g_ref[...])


def _merge(x, h, u, v, ob, oc, lw):
    bsz, seq, _ = x.shape
    tm = min(TM_MERGE, seq)
    tok = pl.BlockSpec((1, tm, D_MODEL), lambda b, i: (b, i, 0))
    consts = [lw["w_gates"], lw["a_w_s"], lw["a_b_s"], lw["w_pa"], lw["w_pb"], lw["w_pc"], lw["w_o"],
              lw["post_mix_g"]]
    return pl.pallas_call(
        _merge_kernel,
        grid=(bsz, seq // tm),
        in_specs=[tok] * 6 + [_const_spec(a) for a in consts],
        out_specs=tok,
        out_shape=jax.ShapeDtypeStruct(x.shape, F32),
        scratch_shapes=[pltpu.VMEM((tm, A_WIDTH), BF16), pltpu.VMEM((tm, D_MODEL), BF16)],
        compiler_params=_params(2),
        name="merge",
    )(x, h, u, v, ob, oc, *consts)


def _ffn_kernel(x_ref, gpre_ref, wg_ref, wu_ref, wd_ref, gpost_ref, out_ref, a_sc):
    x = x_ref[0]
    h = _rms(x, gpre_ref[...]).astype(BF16)
    nc = 2 * LANES
    for c in range(0, D_FF, nc):
        gt = _dot(h, wg_ref[:, c:c + nc])
        a_sc[:, c:c + nc] = (gt * jax.nn.sigmoid(gt) * _dot(h, wu_ref[:, c:c + nc])).astype(BF16)
    f = _dot(a_sc[...], wd_ref[...])
    out_ref[0] = x + _rms(f, gpost_ref[...])


def _ffn(x, lw):
    bsz, seq, _ = x.shape
    tm = min(TM_FFN, seq)
    tok = pl.BlockSpec((1, tm, D_MODEL), lambda b, i: (b, i, 0))
    consts = [lw["pre_ffn_g"], lw["w_gate"], lw["w_up"], lw["w_down"], lw["post_ffn_g"]]
    return pl.pallas_call(
        _ffn_kernel,
        grid=(bsz, seq // tm),
        in_specs=[tok] + [_const_spec(a) for a in consts],
        out_specs=tok,
        out_shape=jax.ShapeDtypeStruct(x.shape, F32),
        scratch_shapes=[pltpu.VMEM((tm, D_FF), BF16)],
        compiler_params=_params(2),
        name="ffn",
    )(x, *consts)


def _rope_tables(seq):
    pos = jnp.arange(seq, dtype=F32)[:, None]

    def tables(dim):
        inv = 1.0 / (ROPE_THETA ** (jnp.arange(0, dim, 2, dtype=F32) / dim))
        ang = pos * inv[None, :]
        return jnp.cos(ang), jnp.sin(ang)

    cb, sb = tables(B_ROPE)
    zb = jnp.zeros((seq, LANES - B_ROPE), F32)
    cc, sc = tables(C_HD)
    return (jnp.concatenate([cb, cb, zb], axis=1), jnp.concatenate([-sb, sb, zb], axis=1),
            jnp.concatenate([cc, cc], axis=1), jnp.concatenate([-sc, sc], axis=1))


def _prep_layer(l, p):
    row = lambda a: a[l][None, :].astype(F32)
    w_in = p["w_in"][l]
    offs = [0]
    for s in IN_SIZES:
        offs.append(offs[-1] + s)
    seg = lambda k: w_in[:, offs[k]:offs[k + 1]]
    w_kr = seg(4)
    half = B_ROPE // 2
    zpad = jnp.zeros((D_MODEL, LANES - B_ROPE), F32)
    w_lat = jnp.concatenate([seg(2), seg(3), w_kr, zpad, w_kr[:, half:], w_kr[:, :half], zpad], axis=1)
    w_uq = p["b_w_uq"][l].reshape(B_Q_RANK, B_HEADS, B_NOPE + B_ROPE)
    q_pe = w_uq[:, :, B_NOPE:]
    zq = jnp.zeros((B_Q_RANK, B_HEADS, LANES - B_ROPE), F32)
    w_uq_main = jnp.concatenate([w_uq[:, :, :B_NOPE], q_pe, zq], axis=2).reshape(B_Q_RANK, B_HEADS * B_QK_PAD)
    w_uq_sw = jnp.concatenate([q_pe[:, :, half:], q_pe[:, :, :half], zq], axis=2).reshape(B_Q_RANK, B_HEADS * LANES)
    w_ukv = p["b_w_ukv"][l].reshape(B_KV_RANK, B_HEADS, B_NOPE + B_VDIM)
    return {
        "pre_mix_g": row(p["pre_mix_g"]),
        "w_uv": jnp.concatenate([seg(0), seg(1)], axis=1).astype(BF16),
        "w_lat": w_lat.astype(BF16),
        "w_c": jnp.concatenate([seg(5), seg(6), seg(7)], axis=1).astype(BF16),
        "w_gates": seg(8).astype(BF16),
        "a_ln_g": row(p["a_ln_g"]),
        "a_ln_b": row(p["a_ln_b"]),
        "b_q_norm_g": row(p["b_q_norm_g"]),
        "b_kv_norm_g": row(p["b_kv_norm_g"]),
        "w_uq": w_uq_main.astype(BF16),
        "w_uq_sw": w_uq_sw.astype(BF16),
        "w_uk": w_ukv[:, :, :B_NOPE].reshape(B_KV_RANK, B_HEADS * B_NOPE).astype(BF16),
        "w_uvv": w_ukv[:, :, B_NOPE:].reshape(B_KV_RANK, B_HEADS * B_VDIM).astype(BF16),
        "a_w_s": p["a_w_s"][l].astype(BF16),
        "a_b_s": jnp.broadcast_to(p["a_b_s"][l][:, :, None], (A_GROUPS, CHUNK, LANES)).astype(F32),
        "c_sink": jnp.broadcast_to((p["c_sink"][l] * LOG2E)[:, None], (C_HEADS, LANES)).astype(F32),
        "w_pa": p["w_pa"][l].astype(BF16),
        "w_pb": p["w_pb"][l].astype(BF16),
        "w_pc": p["w_pc"][l].astype(BF16),
        "w_o": p["w_o"][l].astype(BF16),
        "post_mix_g": row(p["post_mix_g"]),
        "pre_ffn_g": row(p["pre_ffn_g"]),
        "w_gate": p["w_gate"][l].astype(BF16),
        "w_up": p["w_up"][l].astype(BF16),
        "w_down": p["w_down"][l].astype(BF16),
        "post_ffn_g": row(p["post_ffn_g"]),
    }


def _trunk(x, layers):
    rope = _rope_tables(x.shape[1])
    for lw in layers:
        h, u, v, qm, km, vm, cq, ck, cv = _proj(x, lw, rope)
        ob = _mla(qm, km, vm)
        oc = _swa(cq, ck, cv, lw["c_sink"])
        x = _merge(x, h, u, v, ob, oc, lw)
        x = _ffn(x, lw)
    return x


def kernel(x_prompt, x_sample, pre_mix_g, w_in, a_ln_g, a_ln_b, a_w_s, a_b_s, b_q_norm_g, b_w_uq, b_kv_norm_g,
           b_w_ukv, c_sink, w_pa, w_pb, w_pc, w_o, post_mix_g, pre_ffn_g, w_gate, w_up, w_down, post_ffn_g):
    p = dict(pre_mix_g=pre_mix_g, w_in=w_in, a_ln_g=a_ln_g, a_ln_b=a_ln_b, a_w_s=a_w_s, a_b_s=a_b_s,
             b_q_norm_g=b_q_norm_g, b_w_uq=b_w_uq, b_kv_norm_g=b_kv_norm_g, b_w_ukv=b_w_ukv, c_sink=c_sink,
             w_pa=w_pa, w_pb=w_pb, w_pc=w_pc, w_o=w_o, post_mix_g=post_mix_g, pre_ffn_g=pre_ffn_g,
             w_gate=w_gate, w_up=w_up, w_down=w_down, post_ffn_g=post_ffn_g)
    layers = [_prep_layer(l, p) for l in range(w_in.shape[0])]
    return (_trunk(x_prompt, layers), _trunk(x_sample, layers))
```

```python
import functools
import math

import jax
import jax.numpy as jnp
from jax import lax
from jax.experimental import pallas as pl
from jax.experimental.pallas import tpu as pltpu

F32 = jnp.float32
BF16 = jnp.bfloat16

D_MODEL = 1024
EPS = 1e-6
ROPE_THETA = 10000.0
N_BRANCH = 3
CHUNK = 128
A_GROUPS = 8
A_WIDTH = 1024
B_HEADS = 8
B_Q_RANK = 384
B_KV_RANK = 256
B_NOPE = 128
B_ROPE = 64
B_VDIM = 128
C_HEADS = 8
C_KV_HEADS = 2
C_HD = 128
C_WIDTH = C_HEADS * C_HD
C_KV_WIDTH = C_KV_HEADS * C_HD
C_REP = C_HEADS // C_KV_HEADS
WINDOW = 128
D_FF = -(-8 * D_MODEL // (3 * 256)) * 256
IN_SIZES = (A_WIDTH, A_WIDTH, B_Q_RANK, B_KV_RANK, B_ROPE, C_WIDTH, C_KV_WIDTH, C_KV_WIDTH, N_BRANCH * D_MODEL)

LANES = 128
B_QK_PAD = 2 * LANES
LOG2E = math.log2(math.e)
B_QSCALE = (B_NOPE + B_ROPE) ** -0.5 * LOG2E
C_QSCALE = C_HD ** -0.5 * LOG2E

V7X_VMEM_BYTES = 64 * 1024 * 1024
VMEM_LIMIT = V7X_VMEM_BYTES - 8 * 1024 * 1024

TM_PROJ = 512
TM_MERGE = 512
TM_FFN = 512
TQ_MLA = 512
TK_MLA = 512
SWA_BLOCKS = 4


def _dot(a, b):
    return jnp.dot(a, b, preferred_element_type=F32)


def _dot_nt(a, b):
    return lax.dot_general(a, b, (((1,), (1,)), ((), ())), preferred_element_type=F32)


def _rms(x, g):
    return x * lax.rsqrt(jnp.mean(x * x, axis=-1, keepdims=True) + EPS) * g


def _const_spec(arr):
    nd = arr.ndim
    return pl.BlockSpec(arr.shape, lambda *_: (0,) * nd, pipeline_mode=pl.Buffered(1))


def _params(n_axes):
    return pltpu.CompilerParams(dimension_semantics=("arbitrary",) * n_axes, vmem_limit_bytes=VMEM_LIMIT)


def _proj_kernel(x_ref, g_ref, wuv_ref, wlat_ref, wc_ref, lng_ref, lnb_ref, qg_ref, kvg_ref,
                 wuq_ref, wuqs_ref, wuk_ref, wuvv_ref, cb_ref, sb_ref, cc_ref, sc_ref,
                 h_ref, u_ref, v_ref, qm_ref, km_ref, vm_ref, cq_ref, ck_ref, cv_ref):
    h = _rms(x_ref[0], g_ref[...]).astype(BF16)
    h_ref[0] = h

    nc = 2 * LANES
    for c in range(0, A_WIDTH, nc):
        u_ref[0, :, c:c + nc] = jax.nn.gelu(_dot(h, wuv_ref[:, c:c + nc])).astype(BF16)
    v = jax.nn.gelu(_dot(h, wuv_ref[:, A_WIDTH:]))
    vc = v - jnp.mean(v, axis=-1, keepdims=True)
    var = jnp.mean(vc * vc, axis=-1, keepdims=True)
    v_ref[0] = (vc * lax.rsqrt(var + EPS) * lng_ref[...] + lnb_ref[...]).astype(BF16)

    lat = _dot(h, wlat_ref[...])
    o1 = B_Q_RANK
    o2 = o1 + B_KV_RANK
    cqn = _rms(lat[:, :o1], qg_ref[...]).astype(BF16)
    ckvn = _rms(lat[:, o1:o2], kvg_ref[...]).astype(BF16)
    cb = cb_ref[...]
    sb = sb_ref[...]
    kpe = (lat[:, o2:o2 + LANES] * cb + lat[:, o2 + LANES:o2 + 2 * LANES] * sb).astype(BF16)
    for hd in range(B_HEADS):
        c0 = hd * B_QK_PAD
        qh = _dot(cqn, wuq_ref[:, c0:c0 + B_QK_PAD])
        qs = _dot(cqn, wuqs_ref[:, hd * LANES:(hd + 1) * LANES])
        qm_ref[0, :, c0:c0 + LANES] = (qh[:, :LANES] * B_QSCALE).astype(BF16)
        qm_ref[0, :, c0 + LANES:c0 + B_QK_PAD] = ((qh[:, LANES:] * cb + qs * sb) * B_QSCALE).astype(BF16)
        km_ref[0, :, c0 + LANES:c0 + B_QK_PAD] = kpe
    for hp in range(0, B_HEADS, 2):
        kn = _dot(ckvn, wuk_ref[:, hp * LANES:(hp + 2) * LANES])
        km_ref[0, :, hp * B_QK_PAD:hp * B_QK_PAD + LANES] = kn[:, :LANES].astype(BF16)
        km_ref[0, :, (hp + 1) * B_QK_PAD:(hp + 1) * B_QK_PAD + LANES] = kn[:, LANES:].astype(BF16)
        vm_ref[0, :, hp * LANES:(hp + 2) * LANES] = _dot(ckvn, wuvv_ref[:, hp * LANES:(hp + 2) * LANES]).astype(BF16)

    cc = cc_ref[...]
    sc = sc_ref[...]
    for c in range(0, C_WIDTH + C_KV_WIDTH, nc):
        qk = _dot(h, wc_ref[:, c:c + nc])
        for j in range(nc // LANES):
            t = qk[:, j * LANES:(j + 1) * LANES]
            r = t * cc + pltpu.roll(t, C_HD // 2, 1) * sc
            col = c + j * LANES
            if col < C_WIDTH:
                cq_ref[0, :, col:col + LANES] = (r * C_QSCALE).astype(BF16)
            else:
                ck_ref[0, :, col - C_WIDTH:col - C_WIDTH + LANES] = r.astype(BF16)
    cv_ref[0] = _dot(h, wc_ref[:, C_WIDTH + C_KV_WIDTH:]).astype(BF16)


def _proj(x, lw, rope):
    bsz, seq, _ = x.shape
    tm = min(TM_PROJ, seq)
    grid = (bsz, seq // tm)
    tok = lambda w: pl.BlockSpec((1, tm, w), lambda b, i: (b, i, 0))
    pos = pl.BlockSpec((tm, LANES), lambda b, i: (i, 0))
    consts = [lw["pre_mix_g"], lw["w_uv"], lw["w_lat"], lw["w_c"], lw["a_ln_g"], lw["a_ln_b"], lw["b_q_norm_g"],
              lw["b_kv_norm_g"], lw["w_uq"], lw["w_uq_sw"], lw["w_uk"], lw["w_uvv"]]
    widths = [D_MODEL, A_WIDTH, A_WIDTH, B_HEADS * B_QK_PAD, B_HEADS * B_QK_PAD, B_HEADS * B_VDIM,
              C_WIDTH, C_KV_WIDTH, C_KV_WIDTH]
    return pl.pallas_call(
        _proj_kernel,
        grid=grid,
        in_specs=[tok(D_MODEL)] + [_const_spec(a) for a in consts] + [pos] * 4,
        out_specs=[tok(w) for w in widths],
        out_shape=[jax.ShapeDtypeStruct((bsz, seq, w), BF16) for w in widths],
        compiler_params=_params(2),
        name="proj",
    )(x, *consts, *rope)


def _mla_kernel(q_ref, k_ref, v_ref, o_ref, m_sc, l_sc, acc_sc, s0_sc, s1_sc, p0_sc, p1_sc, a0_sc, a1_sc, *, tk):
    n = k_ref.shape[1] // tk
    s_sc = (s0_sc, s1_sc)
    p_sc = (p0_sc, p1_sc)
    a_sc = (a0_sc, a1_sc)
    m_sc[...] = jnp.full(m_sc.shape, -jnp.inf, F32)
    l_sc[...] = jnp.zeros(l_sc.shape, F32)
    acc_sc[...] = jnp.zeros(acc_sc.shape, F32)

    def kv_rows(c):
        return pl.ds(pl.multiple_of(c * tk, tk), tk)

    def scores(c, par):
        s_sc[par][...] = _dot_nt(q_ref[0], k_ref[0, kv_rows(c), :])

    def softmax(par):
        chunks = [s_sc[par][:, c * LANES:(c + 1) * LANES] for c in range(tk // LANES)]
        m_prev = m_sc[...]
        m_new = jnp.maximum(m_prev, jnp.max(functools.reduce(jnp.maximum, chunks), axis=-1, keepdims=True))
        alpha = jnp.exp2(m_prev - m_new)
        ps = [jnp.exp2(c - m_new) for c in chunks]
        l_sc[...] = alpha * l_sc[...] + functools.reduce(jnp.add, ps)
        m_sc[...] = m_new
        a_sc[par][...] = alpha
        for c, pc in enumerate(ps):
            p_sc[par][:, c * LANES:(c + 1) * LANES] = pc.astype(BF16)

    def values(c, par):
        acc_sc[...] = a_sc[par][...] * acc_sc[...] + _dot(p_sc[par][...], v_ref[0, kv_rows(c), :])

    scores(0, 0)
    scores(1, 1)
    softmax(0)

    def body(tt, carry):
        t = 2 * tt
        scores(t + 2, 0)
        softmax(1)
        values(t, 0)
        scores(t + 3, 1)
        softmax(0)
        values(t + 1, 1)
        return carry

    for tt in range((n - 2) // 2):
        body(tt, 0)
    softmax(1)
    values(n - 2, 0)
    values(n - 1, 1)
    o_ref[0] = (acc_sc[...] / jnp.sum(l_sc[...], axis=-1, keepdims=True)).astype(BF16)


def _mla(qm, km, vm):
    bsz, seq, _ = qm.shape
    tq = min(TQ_MLA, seq)
    tk = min(TK_MLA, seq // 2)
    assert (seq // tk) % 2 == 0
    row_stat = pltpu.VMEM((tq, LANES), F32)
    return pl.pallas_call(
        functools.partial(_mla_kernel, tk=tk),
        grid=(bsz, B_HEADS, seq // tq),
        in_specs=[
            pl.BlockSpec((1, tq, B_QK_PAD), lambda b, h, i: (b, i, h)),
            pl.BlockSpec((1, seq, B_QK_PAD), lambda b, h, i: (b, 0, h)),
            pl.BlockSpec((1, seq, B_VDIM), lambda b, h, i: (b, 0, h)),
        ],
        out_specs=pl.BlockSpec((1, tq, B_VDIM), lambda b, h, i: (b, i, h)),
        out_shape=jax.ShapeDtypeStruct((bsz, seq, B_HEADS * B_VDIM), BF16),
        scratch_shapes=[row_stat, row_stat, pltpu.VMEM((tq, B_VDIM), F32),
                        pltpu.VMEM((tq, tk), F32), pltpu.VMEM((tq, tk), F32),
                        pltpu.VMEM((tq, tk), BF16), pltpu.VMEM((tq, tk), BF16), row_stat, row_stat],
        compiler_params=_params(3),
        name="mla",
    )(qm, km, vm)


def _swa_kernel(q_ref, kc_ref, kp_ref, kn_ref, vc_ref, vp_ref, vn_ref, sink_ref, o_ref, *, nblk):
    i = pl.program_id(1)
    last = pl.num_programs(1) - 1
    rows = C_REP * CHUNK
    qi = lax.broadcasted_iota(jnp.int32, (rows, CHUNK), 0) % CHUNK
    kj = lax.broadcasted_iota(jnp.int32, (rows, CHUNK), 1)
    band_prev = kj >= qi
    band_next = kj <= qi
    neg = jnp.float32(-jnp.inf)

    def blk(ref_c, ref_e, n, g, edge_lo):
        cols = slice(g * C_HD, (g + 1) * C_HD)
        if edge_lo:
            return ref_e[0, :, cols] if n == 0 else ref_c[0, (n - 1) * CHUNK:n * CHUNK, cols]
        return ref_e[0, :, cols] if n == nblk - 1 else ref_c[0, (n + 1) * CHUNK:(n + 2) * CHUNK, cols]

    for n in range(nblk):
        tok = slice(n * CHUNK, (n + 1) * CHUNK)
        for g in range(C_KV_HEADS):
            cols = slice(g * C_HD, (g + 1) * C_HD)
            q4 = jnp.concatenate([q_ref[0, tok, (g * C_REP + r) * C_HD:(g * C_REP + r + 1) * C_HD]
                                  for r in range(C_REP)], axis=0)
            sink = jnp.concatenate([jnp.broadcast_to(sink_ref[g * C_REP + r:g * C_REP + r + 1, :], (CHUNK, LANES))
                                    for r in range(C_REP)], axis=0)
            k3 = jnp.concatenate([blk(kc_ref, kp_ref, n, g, True), kc_ref[0, tok, cols],
                                  blk(kc_ref, kn_ref, n, g, False)], axis=0)
            v3 = jnp.concatenate([blk(vc_ref, vp_ref, n, g, True), vc_ref[0, tok, cols],
                                  blk(vc_ref, vn_ref, n, g, False)], axis=0)
            s = _dot_nt(q4, k3)
            ok_prev = band_prev if n > 0 else jnp.logical_and(band_prev, i > 0)
            ok_next = band_next if n < nblk - 1 else jnp.logical_and(band_next, i < last)
            s0 = jnp.where(ok_prev, s[:, :CHUNK], neg)
            s1 = s[:, CHUNK:2 * CHUNK]
            s2 = jnp.where(ok_next, s[:, 2 * CHUNK:], neg)
            m = jnp.max(jnp.maximum(jnp.maximum(s0, s1), s2), axis=-1, keepdims=True)
            m = jnp.maximum(m, sink)
            es = [jnp.exp2(t - m) for t in (s0, s1, s2)]
            den = jnp.sum(es[0] + es[1] + es[2], axis=-1, keepdims=True) + jnp.exp2(sink - m)
            p = jnp.concatenate([e.astype(BF16) for e in es], axis=1)
            o = _dot(p, v3) / den
            for r in range(C_REP):
                hc = (g * C_REP + r) * C_HD
                o_ref[0, tok, hc:hc + C_HD] = o[r * CHUNK:(r + 1) * CHUNK].astype(BF16)


def _swa(cq, ck, cv, sink):
    bsz, seq, _ = cq.shape
    nblk = min(SWA_BLOCKS, seq // CHUNK)
    ts = nblk * CHUNK
    nb = seq // CHUNK
    cur = lambda w: pl.BlockSpec((1, ts, w), lambda b, i: (b, i, 0))
    prev = pl.BlockSpec((1, CHUNK, C_KV_WIDTH), lambda b, i: (b, jnp.maximum(i * nblk - 1, 0), 0))
    nxt = pl.BlockSpec((1, CHUNK, C_KV_WIDTH), lambda b, i: (b, jnp.minimum((i + 1) * nblk, nb - 1), 0))
    return pl.pallas_call(
        functools.partial(_swa_kernel, nblk=nblk),
        grid=(bsz, seq // ts),
        in_specs=[cur(C_WIDTH), cur(C_KV_WIDTH), prev, nxt, cur(C_KV_WIDTH), prev, nxt, _const_spec(sink)],
        out_specs=cur(C_WIDTH),
        out_shape=jax.ShapeDtypeStruct((bsz, seq, C_WIDTH), BF16),
        compiler_params=_params(2),
        name="swa",
    )(cq, ck, ck, ck, cv, cv, cv, sink)


def _merge_kernel(x_ref, h_ref, u_ref, v_ref, ob_ref, oc_ref, wg_ref, ws_ref, bs_ref, wpa_ref, wpb_ref, wpc_ref,
                  wo_ref, g_ref, out_ref, ya_sc, mg_sc):
    tm = x_ref.shape[1]
    for c in range(tm // CHUNK):
        tok = slice(c * CHUNK, (c + 1) * CHUNK)
        for g in range(A_GROUPS):
            cols = slice(g * LANES, (g + 1) * LANES)
            mixed = _dot(ws_ref[g], v_ref[0, tok, cols]) + bs_ref[g]
            ya_sc[tok, cols] = (u_ref[0, tok, cols].astype(F32) * mixed).astype(BF16)
    h = h_ref[0]
    ya = ya_sc[...]
    ob = ob_ref[0]
    oc = oc_ref[0]
    nc = 2 * LANES
    for c in range(0, D_MODEL, nc):
        cols = slice(c, c + nc)
        acc = jax.nn.sigmoid(_dot(h, wg_ref[:, c:c + nc])) * _dot(ya, wpa_ref[:, cols])
        acc += jax.nn.sigmoid(_dot(h, wg_ref[:, D_MODEL + c:D_MODEL + c + nc])) * _dot(ob, wpb_ref[:, cols])
        acc += jax.nn.sigmoid(_dot(h, wg_ref[:, 2 * D_MODEL + c:2 * D_MODEL + c + nc])) * _dot(oc, wpc_ref[:, cols])
        mg_sc[:, cols] = acc.astype(BF16)
    mix = _dot(mg_sc[...], wo_ref[...])
    out_ref[0] = x_ref[0] + _rms(mix, g_ref[...])


def _merge(x, h, u, v, ob, oc, lw):
    bsz, seq, _ = x.shape
    tm = min(TM_MERGE, seq)
    tok = pl.BlockSpec((1, tm, D_MODEL), lambda b, i: (b, i, 0))
    consts = [lw["w_gates"], lw["a_w_s"], lw["a_b_s"], lw["w_pa"], lw["w_pb"], lw["w_pc"], lw["w_o"],
              lw["post_mix_g"]]
    return pl.pallas_call(
        _merge_kernel,
        grid=(bsz, seq // tm),
        in_specs=[tok] * 6 + [_const_spec(a) for a in consts],
        out_specs=tok,
        out_shape=jax.ShapeDtypeStruct(x.shape, F32),
        scratch_shapes=[pltpu.VMEM((tm, A_WIDTH), BF16), pltpu.VMEM((tm, D_MODEL), BF16)],
        compiler_params=_params(2),
        name="merge",
    )(x, h, u, v, ob, oc, *consts)


def _ffn_kernel(x_ref, gpre_ref, wg_ref, wu_ref, wd_ref, gpost_ref, out_ref, a_sc):
    x = x_ref[0]
    h = _rms(x, gpre_ref[...]).astype(BF16)
    nc = 2 * LANES
    for c in range(0, D_FF, nc):
        gt = _dot(h, wg_ref[:, c:c + nc])
        a_sc[:, c:c + nc] = (gt * jax.nn.sigmoid(gt) * _dot(h, wu_ref[:, c:c + nc])).astype(BF16)
    f = _dot(a_sc[...], wd_ref[...])
    out_ref[0] = x + _rms(f, gpost_ref[...])


def _ffn(x, lw):
    bsz, seq, _ = x.shape
    tm = min(TM_FFN, seq)
    tok = pl.BlockSpec((1, tm, D_MODEL), lambda b, i: (b, i, 0))
    consts = [lw["pre_ffn_g"], lw["w_gate"], lw["w_up"], lw["w_down"], lw["post_ffn_g"]]
    return pl.pallas_call(
        _ffn_kernel,
        grid=(bsz, seq // tm),
        in_specs=[tok] + [_const_spec(a) for a in consts],
        out_specs=tok,
        out_shape=jax.ShapeDtypeStruct(x.shape, F32),
        scratch_shapes=[pltpu.VMEM((tm, D_FF), BF16)],
        compiler_params=_params(2),
        name="ffn",
    )(x, *consts)


def _rope_tables(seq):
    pos = jnp.arange(seq, dtype=F32)[:, None]

    def tables(dim):
        inv = 1.0 / (ROPE_THETA ** (jnp.arange(0, dim, 2, dtype=F32) / dim))
        ang = pos * inv[None, :]
        return jnp.cos(ang), jnp.sin(ang)

    cb, sb = tables(B_ROPE)
    zb = jnp.zeros((seq, LANES - B_ROPE), F32)
    cc, sc = tables(C_HD)
    return (jnp.concatenate([cb, cb, zb], axis=1), jnp.concatenate([-sb, sb, zb], axis=1),
            jnp.concatenate([cc, cc], axis=1), jnp.concatenate([-sc, sc], axis=1))


def _prep_layer(l, p):
    row = lambda a: a[l][None, :].astype(F32)
    w_in = p["w_in"][l]
    offs = [0]
    for s in IN_SIZES:
        offs.append(offs[-1] + s)
    seg = lambda k: w_in[:, offs[k]:offs[k + 1]]
    w_kr = seg(4)
    half = B_ROPE // 2
    zpad = jnp.zeros((D_MODEL, LANES - B_ROPE), F32)
    w_lat = jnp.concatenate([seg(2), seg(3), w_kr, zpad, w_kr[:, half:], w_kr[:, :half], zpad], axis=1)
    w_uq = p["b_w_uq"][l].reshape(B_Q_RANK, B_HEADS, B_NOPE + B_ROPE)
    q_pe = w_uq[:, :, B_NOPE:]
    zq = jnp.zeros((B_Q_RANK, B_HEADS, LANES - B_ROPE), F32)
    w_uq_main = jnp.concatenate([w_uq[:, :, :B_NOPE], q_pe, zq], axis=2).reshape(B_Q_RANK, B_HEADS * B_QK_PAD)
    w_uq_sw = jnp.concatenate([q_pe[:, :, half:], q_pe[:, :, :half], zq], axis=2).reshape(B_Q_RANK, B_HEADS * LANES)
    w_ukv = p["b_w_ukv"][l].reshape(B_KV_RANK, B_HEADS, B_NOPE + B_VDIM)
    return {
        "pre_mix_g": row(p["pre_mix_g"]),
        "w_uv": jnp.concatenate([seg(0), seg(1)], axis=1).astype(BF16),
        "w_lat": w_lat.astype(BF16),
        "w_c": jnp.concatenate([seg(5), seg(6), seg(7)], axis=1).astype(BF16),
        "w_gates": seg(8).astype(BF16),
        "a_ln_g": row(p["a_ln_g"]),
        "a_ln_b": row(p["a_ln_b"]),
        "b_q_norm_g": row(p["b_q_norm_g"]),
        "b_kv_norm_g": row(p["b_kv_norm_g"]),
        "w_uq": w_uq_main.astype(BF16),
        "w_uq_sw": w_uq_sw.astype(BF16),
        "w_uk": w_ukv[:, :, :B_NOPE].reshape(B_KV_RANK, B_HEADS * B_NOPE).astype(BF16),
        "w_uvv": w_ukv[:, :, B_NOPE:].reshape(B_KV_RANK, B_HEADS * B_VDIM).astype(BF16),
        "a_w_s": p["a_w_s"][l].astype(BF16),
        "a_b_s": jnp.broadcast_to(p["a_b_s"][l][:, :, None], (A_GROUPS, CHUNK, LANES)).astype(F32),
        "c_sink": jnp.broadcast_to((p["c_sink"][l] * LOG2E)[:, None], (C_HEADS, LANES)).astype(F32),
        "w_pa": p["w_pa"][l].astype(BF16),
        "w_pb": p["w_pb"][l].astype(BF16),
        "w_pc": p["w_pc"][l].astype(BF16),
        "w_o": p["w_o"][l].astype(BF16),
        "post_mix_g": row(p["post_mix_g"]),
        "pre_ffn_g": row(p["pre_ffn_g"]),
        "w_gate": p["w_gate"][l].astype(BF16),
        "w_up": p["w_up"][l].astype(BF16),
        "w_down": p["w_down"][l].astype(BF16),
        "post_ffn_g": row(p["post_ffn_g"]),
    }


def _trunk(x, layers):
    rope = _rope_tables(x.shape[1])
    for lw in layers:
        h, u, v, qm, km, vm, cq, ck, cv = _proj(x, lw, rope)
        ob = _mla(qm, km, vm)
        oc = _swa(cq, ck, cv, lw["c_sink"])
        x = _merge(x, h, u, v, ob, oc, lw)
        x = _ffn(x, lw)
    return x


def kernel(x_prompt, x_sample, pre_mix_g, w_in, a_ln_g, a_ln_b, a_w_s, a_b_s, b_q_norm_g, b_w_uq, b_kv_norm_g,
           b_w_ukv, c_sink, w_pa, w_pb, w_pc, w_o, post_mix_g, pre_ffn_g, w_gate, w_up, w_down, post_ffn_g):
    p = dict(pre_mix_g=pre_mix_g, w_in=w_in, a_ln_g=a_ln_g, a_ln_b=a_ln_b, a_w_s=a_w_s, a_b_s=a_b_s,
             b_q_norm_g=b_q_norm_g, b_w_uq=b_w_uq, b_kv_norm_g=b_kv_norm_g, b_w_ukv=b_w_ukv, c_sink=c_sink,
             w_pa=w_pa, w_pb=w_pb, w_pc=w_pc, w_o=w_o, post_mix_g=post_mix_g, pre_ffn_g=pre_ffn_g,
             w_gate=w_gate, w_up=w_up, w_down=w_down, post_ffn_g=post_ffn_g)
    layers = [_prep_layer(l, p) for l in range(w_in.shape[0])]
    return (_trunk(x_prompt, layers), _trunk(x_sample, layers))
```

```python
import functools
import math

import jax
import jax.numpy as jnp
from jax import lax
from jax.experimental import pallas as pl
from jax.experimental.pallas import tpu as pltpu

F32 = jnp.float32
BF16 = jnp.bfloat16

D_MODEL = 1024
EPS = 1e-6
ROPE_THETA = 10000.0
N_BRANCH = 3
CHUNK = 128
A_GROUPS = 8
A_WIDTH = 1024
B_HEADS = 8
B_Q_RANK = 384
B_KV_RANK = 256
B_NOPE = 128
B_ROPE = 64
B_VDIM = 128
C_HEADS = 8
C_KV_HEADS = 2
C_HD = 128
C_WIDTH = C_HEADS * C_HD
C_KV_WIDTH = C_KV_HEADS * C_HD
C_REP = C_HEADS // C_KV_HEADS
WINDOW = 128
D_FF = -(-8 * D_MODEL // (3 * 256)) * 256
IN_SIZES = (A_WIDTH, A_WIDTH, B_Q_RANK, B_KV_RANK, B_ROPE, C_WIDTH, C_KV_WIDTH, C_KV_WIDTH, N_BRANCH * D_MODEL)

LANES = 128
B_QK_PAD = 2 * LANES
LOG2E = math.log2(math.e)
B_QSCALE = (B_NOPE + B_ROPE) ** -0.5 * LOG2E
C_QSCALE = C_HD ** -0.5 * LOG2E

V7X_VMEM_BYTES = 64 * 1024 * 1024
VMEM_LIMIT = V7X_VMEM_BYTES - 8 * 1024 * 1024

TM_PROJ = 512
TM_MERGE = 512
TM_FFN = 512
TQ_MLA = 512
TK_MLA = 512
MLA_QBLOCKS = 2
SWA_BLOCKS = 4


def _dot(a, b):
    return jnp.dot(a, b, preferred_element_type=F32)


def _dot_nt(a, b):
    return lax.dot_general(a, b, (((1,), (1,)), ((), ())), preferred_element_type=F32)


def _rms(x, g):
    return x * lax.rsqrt(jnp.mean(x * x, axis=-1, keepdims=True) + EPS) * g


def _const_spec(arr):
    nd = arr.ndim
    return pl.BlockSpec(arr.shape, lambda *_: (0,) * nd, pipeline_mode=pl.Buffered(1))


def _params(n_axes):
    return pltpu.CompilerParams(dimension_semantics=("arbitrary",) * n_axes, vmem_limit_bytes=VMEM_LIMIT)


def _proj_kernel(x_ref, g_ref, wuv_ref, wlat_ref, wc_ref, lng_ref, lnb_ref, qg_ref, kvg_ref,
                 wuq_ref, wuqs_ref, wuk_ref, wuvv_ref, cb_ref, sb_ref, cc_ref, sc_ref,
                 h_ref, u_ref, v_ref, qm_ref, km_ref, vm_ref, cq_ref, ck_ref, cv_ref):
    h = _rms(x_ref[0], g_ref[...]).astype(BF16)
    h_ref[0] = h

    nc = 2 * LANES
    for c in range(0, A_WIDTH, nc):
        u_ref[0, :, c:c + nc] = jax.nn.gelu(_dot(h, wuv_ref[:, c:c + nc])).astype(BF16)
    v = jax.nn.gelu(_dot(h, wuv_ref[:, A_WIDTH:]))
    vc = v - jnp.mean(v, axis=-1, keepdims=True)
    var = jnp.mean(vc * vc, axis=-1, keepdims=True)
    v_ref[0] = (vc * lax.rsqrt(var + EPS) * lng_ref[...] + lnb_ref[...]).astype(BF16)

    lat = _dot(h, wlat_ref[...])
    o1 = B_Q_RANK
    o2 = o1 + B_KV_RANK
    cqn = _rms(lat[:, :o1], qg_ref[...]).astype(BF16)
    ckvn = _rms(lat[:, o1:o2], kvg_ref[...]).astype(BF16)
    cb = cb_ref[...]
    sb = sb_ref[...]
    kpe = (lat[:, o2:o2 + LANES] * cb + lat[:, o2 + LANES:o2 + 2 * LANES] * sb).astype(BF16)
    for hd in range(B_HEADS):
        c0 = hd * B_QK_PAD
        qh = _dot(cqn, wuq_ref[:, c0:c0 + B_QK_PAD])
        qs = _dot(cqn, wuqs_ref[:, hd * LANES:(hd + 1) * LANES])
        qm_ref[0, :, c0:c0 + LANES] = (qh[:, :LANES] * B_QSCALE).astype(BF16)
        qm_ref[0, :, c0 + LANES:c0 + B_QK_PAD] = ((qh[:, LANES:] * cb + qs * sb) * B_QSCALE).astype(BF16)
        km_ref[0, :, c0 + LANES:c0 + B_QK_PAD] = kpe
    for hp in range(0, B_HEADS, 2):
        kn = _dot(ckvn, wuk_ref[:, hp * LANES:(hp + 2) * LANES])
        km_ref[0, :, hp * B_QK_PAD:hp * B_QK_PAD + LANES] = kn[:, :LANES].astype(BF16)
        km_ref[0, :, (hp + 1) * B_QK_PAD:(hp + 1) * B_QK_PAD + LANES] = kn[:, LANES:].astype(BF16)
        vm_ref[0, :, hp * LANES:(hp + 2) * LANES] = _dot(ckvn, wuvv_ref[:, hp * LANES:(hp + 2) * LANES]).astype(BF16)

    cc = cc_ref[...]
    sc = sc_ref[...]
    for c in range(0, C_WIDTH + C_KV_WIDTH, nc):
        qk = _dot(h, wc_ref[:, c:c + nc])
        for j in range(nc // LANES):
            t = qk[:, j * LANES:(j + 1) * LANES]
            r = t * cc + pltpu.roll(t, C_HD // 2, 1) * sc
            col = c + j * LANES
            if col < C_WIDTH:
                cq_ref[0, :, col:col + LANES] = (r * C_QSCALE).astype(BF16)
            else:
                ck_ref[0, :, col - C_WIDTH:col - C_WIDTH + LANES] = r.astype(BF16)
    cv_ref[0] = _dot(h, wc_ref[:, C_WIDTH + C_KV_WIDTH:]).astype(BF16)


def _proj(x, lw, rope):
    bsz, seq, _ = x.shape
    tm = min(TM_PROJ, seq)
    grid = (bsz, seq // tm)
    tok = lambda w: pl.BlockSpec((1, tm, w), lambda b, i: (b, i, 0))
    pos = pl.BlockSpec((tm, LANES), lambda b, i: (i, 0))
    consts = [lw["pre_mix_g"], lw["w_uv"], lw["w_lat"], lw["w_c"], lw["a_ln_g"], lw["a_ln_b"], lw["b_q_norm_g"],
              lw["b_kv_norm_g"], lw["w_uq"], lw["w_uq_sw"], lw["w_uk"], lw["w_uvv"]]
    widths = [D_MODEL, A_WIDTH, A_WIDTH, B_HEADS * B_QK_PAD, B_HEADS * B_QK_PAD, B_HEADS * B_VDIM,
              C_WIDTH, C_KV_WIDTH, C_KV_WIDTH]
    return pl.pallas_call(
        _proj_kernel,
        grid=grid,
        in_specs=[tok(D_MODEL)] + [_const_spec(a) for a in consts] + [pos] * 4,
        out_specs=[tok(w) for w in widths],
        out_shape=[jax.ShapeDtypeStruct((bsz, seq, w), BF16) for w in widths],
        compiler_params=_params(2),
        name="proj",
    )(x, *consts, *rope)


def _mla_kernel(q_ref, k_ref, v_ref, o_ref, *scratch, tq, tk):
    per_block = len(scratch) // MLA_QBLOCKS
    for qb in range(MLA_QBLOCKS):
        rows = slice(qb * tq, (qb + 1) * tq)
        _mla_block(q_ref.at[0, rows, :], k_ref, v_ref, o_ref.at[0, rows, :],
                   *scratch[qb * per_block:(qb + 1) * per_block], tk=tk)


def _mla_block(q_ref, k_ref, v_ref, o_ref, m_sc, acc_sc, s0_sc, s1_sc, p0_sc, p1_sc, a0_sc, a1_sc, *, tk):
    n = k_ref.shape[1] // tk
    s_sc = (s0_sc, s1_sc)
    p_sc = (p0_sc, p1_sc)
    a_sc = (a0_sc, a1_sc)
    m_sc[...] = jnp.full(m_sc.shape, -jnp.inf, F32)
    acc_sc[...] = jnp.zeros(acc_sc.shape, F32)
    ones = jnp.ones((tk, LANES), BF16)

    def scores(c, par):
        s_sc[par][...] = _dot_nt(q_ref[...], k_ref[0, c * tk:(c + 1) * tk, :])

    def softmax(par):
        chunks = [s_sc[par][:, c * LANES:(c + 1) * LANES] for c in range(tk // LANES)]
        m_prev = m_sc[...]
        m_new = jnp.maximum(m_prev, jnp.max(functools.reduce(jnp.maximum, chunks), axis=-1, keepdims=True))
        m_sc[...] = m_new
        a_sc[par][...] = jnp.exp2(m_prev - m_new)
        for c, sc in enumerate(chunks):
            p_sc[par][:, c * LANES:(c + 1) * LANES] = jnp.exp2((sc - m_new).astype(BF16))

    def values(c, par):
        v1 = jnp.concatenate([v_ref[0, c * tk:(c + 1) * tk, :], ones], axis=1)
        alpha = a_sc[par][...]
        acc_sc[...] = jnp.concatenate([alpha, alpha], axis=1) * acc_sc[...] + _dot(p_sc[par][...], v1)

    scores(0, 0)
    scores(1, 1)
    softmax(0)
    for t in range(n - 2):
        scores(t + 2, t % 2)
        softmax((t + 1) % 2)
        values(t, t % 2)
    softmax((n - 1) % 2)
    values(n - 2, n % 2)
    values(n - 1, (n - 1) % 2)
    o_ref[...] = (acc_sc[:, :B_VDIM] / acc_sc[:, B_VDIM:]).astype(BF16)


def _mla(qm, km, vm):
    bsz, seq, _ = qm.shape
    tq = min(TQ_MLA, seq // MLA_QBLOCKS)
    tk = min(TK_MLA, seq // 2)
    tstep = tq * MLA_QBLOCKS
    row_stat = pltpu.VMEM((tq, LANES), F32)
    block_scratch = [row_stat, pltpu.VMEM((tq, B_VDIM + LANES), F32),
                     pltpu.VMEM((tq, tk), F32), pltpu.VMEM((tq, tk), F32),
                     pltpu.VMEM((tq, tk), BF16), pltpu.VMEM((tq, tk), BF16), row_stat, row_stat]
    return pl.pallas_call(
        functools.partial(_mla_kernel, tq=tq, tk=tk),
        grid=(bsz, B_HEADS, seq // tstep),
        in_specs=[
            pl.BlockSpec((1, tstep, B_QK_PAD), lambda b, h, i: (b, i, h)),
            pl.BlockSpec((1, seq, B_QK_PAD), lambda b, h, i: (b, 0, h)),
            pl.BlockSpec((1, seq, B_VDIM), lambda b, h, i: (b, 0, h)),
        ],
        out_specs=pl.BlockSpec((1, tstep, B_VDIM), lambda b, h, i: (b, i, h)),
        out_shape=jax.ShapeDtypeStruct((bsz, seq, B_HEADS * B_VDIM), BF16),
        scratch_shapes=block_scratch * MLA_QBLOCKS,
        compiler_params=_params(3),
        name="mla",
    )(qm, km, vm)


def _swa_kernel(q_ref, kc_ref, kp_ref, kn_ref, vc_ref, vp_ref, vn_ref, sink_ref, o_ref, *, nblk):
    i = pl.program_id(1)
    last = pl.num_programs(1) - 1
    rows = C_REP * CHUNK
    qi = lax.broadcasted_iota(jnp.int32, (rows, CHUNK), 0) % CHUNK
    kj = lax.broadcasted_iota(jnp.int32, (rows, CHUNK), 1)
    band_prev = kj >= qi
    band_next = kj <= qi
    neg = jnp.float32(-jnp.inf)

    def blk(ref_c, ref_e, n, g, edge_lo):
        cols = slice(g * C_HD, (g + 1) * C_HD)
        if edge_lo:
            return ref_e[0, :, cols] if n == 0 else ref_c[0, (n - 1) * CHUNK:n * CHUNK, cols]
        return ref_e[0, :, cols] if n == nblk - 1 else ref_c[0, (n + 1) * CHUNK:(n + 2) * CHUNK, cols]

    for n in range(nblk):
        tok = slice(n * CHUNK, (n + 1) * CHUNK)
        for g in range(C_KV_HEADS):
            cols = slice(g * C_HD, (g + 1) * C_HD)
            q4 = jnp.concatenate([q_ref[0, tok, (g * C_REP + r) * C_HD:(g * C_REP + r + 1) * C_HD]
                                  for r in range(C_REP)], axis=0)
            sink = jnp.concatenate([jnp.broadcast_to(sink_ref[g * C_REP + r:g * C_REP + r + 1, :], (CHUNK, LANES))
                                    for r in range(C_REP)], axis=0)
            k3 = jnp.concatenate([blk(kc_ref, kp_ref, n, g, True), kc_ref[0, tok, cols],
                                  blk(kc_ref, kn_ref, n, g, False)], axis=0)
            v3 = jnp.concatenate([blk(vc_ref, vp_ref, n, g, True), vc_ref[0, tok, cols],
                                  blk(vc_ref, vn_ref, n, g, False)], axis=0)
            s = _dot_nt(q4, k3)
            ok_prev = band_prev if n > 0 else jnp.logical_and(band_prev, i > 0)
            ok_next = band_next if n < nblk - 1 else jnp.logical_and(band_next, i < last)
            s0 = jnp.where(ok_prev, s[:, :CHUNK], neg)
            s1 = s[:, CHUNK:2 * CHUNK]
            s2 = jnp.where(ok_next, s[:, 2 * CHUNK:], neg)
            m = jnp.max(jnp.maximum(jnp.maximum(s0, s1), s2), axis=-1, keepdims=True)
            m = jnp.maximum(m, sink)
            es = [jnp.exp2(t - m) for t in (s0, s1, s2)]
            den = jnp.sum(es[0] + es[1] + es[2], axis=-1, keepdims=True) + jnp.exp2(sink - m)
            p = jnp.concatenate([e.astype(BF16) for e in es], axis=1)
            o = _dot(p, v3) / den
            for r in range(C_REP):
                hc = (g * C_REP + r) * C_HD
                o_ref[0, tok, hc:hc + C_HD] = o[r * CHUNK:(r + 1) * CHUNK].astype(BF16)


def _swa(cq, ck, cv, sink):
    bsz, seq, _ = cq.shape
    nblk = min(SWA_BLOCKS, seq // CHUNK)
    ts = nblk * CHUNK
    nb = seq // CHUNK
    cur = lambda w: pl.BlockSpec((1, ts, w), lambda b, i: (b, i, 0))
    prev = pl.BlockSpec((1, CHUNK, C_KV_WIDTH), lambda b, i: (b, jnp.maximum(i * nblk - 1, 0), 0))
    nxt = pl.BlockSpec((1, CHUNK, C_KV_WIDTH), lambda b, i: (b, jnp.minimum((i + 1) * nblk, nb - 1), 0))
    return pl.pallas_call(
        functools.partial(_swa_kernel, nblk=nblk),
        grid=(bsz, seq // ts),
        in_specs=[cur(C_WIDTH), cur(C_KV_WIDTH), prev, nxt, cur(C_KV_WIDTH), prev, nxt, _const_spec(sink)],
        out_specs=cur(C_WIDTH),
        out_shape=jax.ShapeDtypeStruct((bsz, seq, C_WIDTH), BF16),
        compiler_params=_params(2),
        name="swa",
    )(cq, ck, ck, ck, cv, cv, cv, sink)


def _merge_kernel(x_ref, h_ref, u_ref, v_ref, ob_ref, oc_ref, wg_ref, ws_ref, bs_ref, wpa_ref, wpb_ref, wpc_ref,
                  wo_ref, g_ref, out_ref, ya_sc, mg_sc):
    tm = x_ref.shape[1]
    for c in range(tm // CHUNK):
        tok = slice(c * CHUNK, (c + 1) * CHUNK)
        for g in range(A_GROUPS):
            cols = slice(g * LANES, (g + 1) * LANES)
            mixed = _dot(ws_ref[g], v_ref[0, tok, cols]) + bs_ref[g]
            ya_sc[tok, cols] = (u_ref[0, tok, cols].astype(F32) * mixed).astype(BF16)
    h = h_ref[0]
    ya = ya_sc[...]
    ob = ob_ref[0]
    oc = oc_ref[0]
    nc = 2 * LANES
    for c in range(0, D_MODEL, nc):
        cols = slice(c, c + nc)
        acc = jax.nn.sigmoid(_dot(h, wg_ref[:, c:c + nc])) * _dot(ya, wpa_ref[:, cols])
        acc += jax.nn.sigmoid(_dot(h, wg_ref[:, D_MODEL + c:D_MODEL + c + nc])) * _dot(ob, wpb_ref[:, cols])
        acc += jax.nn.sigmoid(_dot(h, wg_ref[:, 2 * D_MODEL + c:2 * D_MODEL + c + nc])) * _dot(oc, wpc_ref[:, cols])
        mg_sc[:, cols] = acc.astype(BF16)
    mix = _dot(mg_sc[...], wo_ref[...])
    out_ref[0] = x_ref[0] + _rms(mix, g_ref[...])


def _merge(x, h, u, v, ob, oc, lw):
    bsz, seq, _ = x.shape
    tm = min(TM_MERGE, seq)
    tok = pl.BlockSpec((1, tm, D_MODEL), lambda b, i: (b, i, 0))
    consts = [lw["w_gates"], lw["a_w_s"], lw["a_b_s"], lw["w_pa"], lw["w_pb"], lw["w_pc"], lw["w_o"],
              lw["post_mix_g"]]
    return pl.pallas_call(
        _merge_kernel,
        grid=(bsz, seq // tm),
        in_specs=[tok] * 6 + [_const_spec(a) for a in consts],
        out_specs=tok,
        out_shape=jax.ShapeDtypeStruct(x.shape, F32),
        scratch_shapes=[pltpu.VMEM((tm, A_WIDTH), BF16), pltpu.VMEM((tm, D_MODEL), BF16)],
        compiler_params=_params(2),
        name="merge",
    )(x, h, u, v, ob, oc, *consts)


def _ffn_kernel(x_ref, gpre_ref, wg_ref, wu_ref, wd_ref, gpost_ref, out_ref, a_sc):
    x = x_ref[0]
    h = _rms(x, gpre_ref[...]).astype(BF16)
    nc = 2 * LANES
    for c in range(0, D_FF, nc):
        gt = _dot(h, wg_ref[:, c:c + nc])
        a_sc[:, c:c + nc] = (gt * jax.nn.sigmoid(gt) * _dot(h, wu_ref[:, c:c + nc])).astype(BF16)
    f = _dot(a_sc[...], wd_ref[...])
    out_ref[0] = x + _rms(f, gpost_ref[...])


def _ffn(x, lw):
    bsz, seq, _ = x.shape
    tm = min(TM_FFN, seq)
    tok = pl.BlockSpec((1, tm, D_MODEL), lambda b, i: (b, i, 0))
    consts = [lw["pre_ffn_g"], lw["w_gate"], lw["w_up"], lw["w_down"], lw["post_ffn_g"]]
    return pl.pallas_call(
        _ffn_kernel,
        grid=(bsz, seq // tm),
        in_specs=[tok] + [_const_spec(a) for a in consts],
        out_specs=tok,
        out_shape=jax.ShapeDtypeStruct(x.shape, F32),
        scratch_shapes=[pltpu.VMEM((tm, D_FF), BF16)],
        compiler_params=_params(2),
        name="ffn",
    )(x, *consts)


def _rope_tables(seq):
    pos = jnp.arange(seq, dtype=F32)[:, None]

    def tables(dim):
        inv = 1.0 / (ROPE_THETA ** (jnp.arange(0, dim, 2, dtype=F32) / dim))
        ang = pos * inv[None, :]
        return jnp.cos(ang), jnp.sin(ang)

    cb, sb = tables(B_ROPE)
    zb = jnp.zeros((seq, LANES - B_ROPE), F32)
    cc, sc = tables(C_HD)
    return (jnp.concatenate([cb, cb, zb], axis=1), jnp.concatenate([-sb, sb, zb], axis=1),
            jnp.concatenate([cc, cc], axis=1), jnp.concatenate([-sc, sc], axis=1))


def _prep_layer(l, p):
    row = lambda a: a[l][None, :].astype(F32)
    w_in = p["w_in"][l]
    offs = [0]
    for s in IN_SIZES:
        offs.append(offs[-1] + s)
    seg = lambda k: w_in[:, offs[k]:offs[k + 1]]
    w_kr = seg(4)
    half = B_ROPE // 2
    zpad = jnp.zeros((D_MODEL, LANES - B_ROPE), F32)
    w_lat = jnp.concatenate([seg(2), seg(3), w_kr, zpad, w_kr[:, half:], w_kr[:, :half], zpad], axis=1)
    w_uq = p["b_w_uq"][l].reshape(B_Q_RANK, B_HEADS, B_NOPE + B_ROPE)
    q_pe = w_uq[:, :, B_NOPE:]
    zq = jnp.zeros((B_Q_RANK, B_HEADS, LANES - B_ROPE), F32)
    w_uq_main = jnp.concatenate([w_uq[:, :, :B_NOPE], q_pe, zq], axis=2).reshape(B_Q_RANK, B_HEADS * B_QK_PAD)
    w_uq_sw = jnp.concatenate([q_pe[:, :, half:], q_pe[:, :, :half], zq], axis=2).reshape(B_Q_RANK, B_HEADS * LANES)
    w_ukv = p["b_w_ukv"][l].reshape(B_KV_RANK, B_HEADS, B_NOPE + B_VDIM)
    return {
        "pre_mix_g": row(p["pre_mix_g"]),
        "w_uv": jnp.concatenate([seg(0), seg(1)], axis=1).astype(BF16),
        "w_lat": w_lat.astype(BF16),
        "w_c": jnp.concatenate([seg(5), seg(6), seg(7)], axis=1).astype(BF16),
        "w_gates": seg(8).astype(BF16),
        "a_ln_g": row(p["a_ln_g"]),
        "a_ln_b": row(p["a_ln_b"]),
        "b_q_norm_g": row(p["b_q_norm_g"]),
        "b_kv_norm_g": row(p["b_kv_norm_g"]),
        "w_uq": w_uq_main.astype(BF16),
        "w_uq_sw": w_uq_sw.astype(BF16),
        "w_uk": w_ukv[:, :, :B_NOPE].reshape(B_KV_RANK, B_HEADS * B_NOPE).astype(BF16),
        "w_uvv": w_ukv[:, :, B_NOPE:].reshape(B_KV_RANK, B_HEADS * B_VDIM).astype(BF16),
        "a_w_s": p["a_w_s"][l].astype(BF16),
        "a_b_s": jnp.broadcast_to(p["a_b_s"][l][:, :, None], (A_GROUPS, CHUNK, LANES)).astype(F32),
        "c_sink": jnp.broadcast_to((p["c_sink"][l] * LOG2E)[:, None], (C_HEADS, LANES)).astype(F32),
        "w_pa": p["w_pa"][l].astype(BF16),
        "w_pb": p["w_pb"][l].astype(BF16),
        "w_pc": p["w_pc"][l].astype(BF16),
        "w_o": p["w_o"][l].astype(BF16),
        "post_mix_g": row(p["post_mix_g"]),
        "pre_ffn_g": row(p["pre_ffn_g"]),
        "w_gate": p["w_gate"][l].astype(BF16),
        "w_up": p["w_up"][l].astype(BF16),
        "w_down": p["w_down"][l].astype(BF16),
        "post_ffn_g": row(p["post_ffn_g"]),
    }


def _trunk(x, layers):
    rope = _rope_tables(x.shape[1])
    for lw in layers:
        h, u, v, qm, km, vm, cq, ck, cv = _proj(x, lw, rope)
        ob = _mla(qm, km, vm)
        oc = _swa(cq, ck, cv, lw["c_sink"])
        x = _merge(x, h, u, v, ob, oc, lw)
        x = _ffn(x, lw)
    return x


def kernel(x_prompt, x_sample, pre_mix_g, w_in, a_ln_g, a_ln_b, a_w_s, a_b_s, b_q_norm_g, b_w_uq, b_kv_norm_g,
           b_w_ukv, c_sink, w_pa, w_pb, w_pc, w_o, post_mix_g, pre_ffn_g, w_gate, w_up, w_down, post_ffn_g):
    p = dict(pre_mix_g=pre_mix_g, w_in=w_in, a_ln_g=a_ln_g, a_ln_b=a_ln_b, a_w_s=a_w_s, a_b_s=a_b_s,
             b_q_norm_g=b_q_norm_g, b_w_uq=b_w_uq, b_kv_norm_g=b_kv_norm_g, b_w_ukv=b_w_ukv, c_sink=c_sink,
             w_pa=w_pa, w_pb=w_pb, w_pc=w_pc, w_o=w_o, post_mix_g=post_mix_g, pre_ffn_g=pre_ffn_g,
             w_gate=w_gate, w_up=w_up, w_down=w_down, post_ffn_g=post_ffn_g)
    layers = [_prep_layer(l, p) for l in range(w_in.shape[0])]
    return (_trunk(x_prompt, layers), _trunk(x_sample, layers))
```

```python
import functools
import math

import jax
import jax.numpy as jnp
from jax import lax
from jax.experimental import pallas as pl
from jax.experimental.pallas import tpu as pltpu

F32 = jnp.float32
BF16 = jnp.bfloat16

D_MODEL = 1024
EPS = 1e-6
ROPE_THETA = 10000.0
N_BRANCH = 3
CHUNK = 128
A_GROUPS = 8
A_WIDTH = 1024
B_HEADS = 8
B_Q_RANK = 384
B_KV_RANK = 256
B_NOPE = 128
B_ROPE = 64
B_VDIM = 128
C_HEADS = 8
C_KV_HEADS = 2
C_HD = 128
C_WIDTH = C_HEADS * C_HD
C_KV_WIDTH = C_KV_HEADS * C_HD
C_REP = C_HEADS // C_KV_HEADS
WINDOW = 128
D_FF = -(-8 * D_MODEL // (3 * 256)) * 256
IN_SIZES = (A_WIDTH, A_WIDTH, B_Q_RANK, B_KV_RANK, B_ROPE, C_WIDTH, C_KV_WIDTH, C_KV_WIDTH, N_BRANCH * D_MODEL)

LANES = 128
B_QK_PAD = 2 * LANES
LOG2E = math.log2(math.e)
B_QSCALE = (B_NOPE + B_ROPE) ** -0.5 * LOG2E
C_QSCALE = C_HD ** -0.5 * LOG2E

V7X_VMEM_BYTES = 64 * 1024 * 1024
VMEM_LIMIT = V7X_VMEM_BYTES - 8 * 1024 * 1024

TM_PROJ = 512
TM_MERGE = 512
TM_FFN = 512
TQ_MLA = 512
TK_MLA = 512
MLA_QBLOCKS = 2
SWA_BLOCKS = 4


def _dot(a, b):
    return jnp.dot(a, b, preferred_element_type=F32)


def _dot_nt(a, b):
    return lax.dot_general(a, b, (((1,), (1,)), ((), ())), preferred_element_type=F32)


def _rms(x, g):
    return x * lax.rsqrt(jnp.mean(x * x, axis=-1, keepdims=True) + EPS) * g


def _const_spec(arr):
    nd = arr.ndim
    return pl.BlockSpec(arr.shape, lambda *_: (0,) * nd, pipeline_mode=pl.Buffered(1))


def _params(n_axes):
    return pltpu.CompilerParams(dimension_semantics=("arbitrary",) * n_axes, vmem_limit_bytes=VMEM_LIMIT)


def _proj_kernel(x_ref, g_ref, wuv_ref, wlat_ref, wc_ref, lng_ref, lnb_ref, qg_ref, kvg_ref,
                 wuq_ref, wuqs_ref, wuk_ref, wuvv_ref, cb_ref, sb_ref, cc_ref, sc_ref,
                 h_ref, u_ref, v_ref, qm_ref, km_ref, vm_ref, cq_ref, ck_ref, cv_ref):
    h = _rms(x_ref[0], g_ref[...]).astype(BF16)
    h_ref[0] = h

    nc = 2 * LANES
    for c in range(0, A_WIDTH, nc):
        u_ref[0, :, c:c + nc] = jax.nn.gelu(_dot(h, wuv_ref[:, c:c + nc])).astype(BF16)
    v = jax.nn.gelu(_dot(h, wuv_ref[:, A_WIDTH:]))
    vc = v - jnp.mean(v, axis=-1, keepdims=True)
    var = jnp.mean(vc * vc, axis=-1, keepdims=True)
    v_ref[0] = (vc * lax.rsqrt(var + EPS) * lng_ref[...] + lnb_ref[...]).astype(BF16)

    lat = _dot(h, wlat_ref[...])
    o1 = B_Q_RANK
    o2 = o1 + B_KV_RANK
    cqn = _rms(lat[:, :o1], qg_ref[...]).astype(BF16)
    ckvn = _rms(lat[:, o1:o2], kvg_ref[...]).astype(BF16)
    cb = cb_ref[...]
    sb = sb_ref[...]
    kpe = (lat[:, o2:o2 + LANES] * cb + lat[:, o2 + LANES:o2 + 2 * LANES] * sb).astype(BF16)
    for hp in range(0, B_HEADS, 2):
        qs2 = _dot(cqn, wuqs_ref[:, hp * LANES:(hp + 2) * LANES])
        for j in range(2):
            c0 = (hp + j) * B_QK_PAD
            qh = _dot(cqn, wuq_ref[:, c0:c0 + B_QK_PAD])
            qs = qs2[:, j * LANES:(j + 1) * LANES]
            qm_ref[0, :, c0:c0 + LANES] = (qh[:, :LANES] * B_QSCALE).astype(BF16)
            qm_ref[0, :, c0 + LANES:c0 + B_QK_PAD] = ((qh[:, LANES:] * cb + qs * sb) * B_QSCALE).astype(BF16)
            km_ref[0, :, c0 + LANES:c0 + B_QK_PAD] = kpe
        kn = _dot(ckvn, wuk_ref[:, hp * LANES:(hp + 2) * LANES])
        km_ref[0, :, hp * B_QK_PAD:hp * B_QK_PAD + LANES] = kn[:, :LANES].astype(BF16)
        km_ref[0, :, (hp + 1) * B_QK_PAD:(hp + 1) * B_QK_PAD + LANES] = kn[:, LANES:].astype(BF16)
        vm_ref[0, :, hp * LANES:(hp + 2) * LANES] = _dot(ckvn, wuvv_ref[:, hp * LANES:(hp + 2) * LANES]).astype(BF16)

    cc = cc_ref[...]
    sc = sc_ref[...]
    for c in range(0, C_WIDTH + C_KV_WIDTH, nc):
        qk = _dot(h, wc_ref[:, c:c + nc])
        for j in range(nc // LANES):
            t = qk[:, j * LANES:(j + 1) * LANES]
            r = t * cc + pltpu.roll(t, C_HD // 2, 1) * sc
            col = c + j * LANES
            if col < C_WIDTH:
                cq_ref[0, :, col:col + LANES] = (r * C_QSCALE).astype(BF16)
            else:
                ck_ref[0, :, col - C_WIDTH:col - C_WIDTH + LANES] = r.astype(BF16)
    cv_ref[0] = _dot(h, wc_ref[:, C_WIDTH + C_KV_WIDTH:]).astype(BF16)


def _proj(x, lw, rope):
    bsz, seq, _ = x.shape
    tm = min(TM_PROJ, seq)
    grid = (bsz, seq // tm)
    tok = lambda w: pl.BlockSpec((1, tm, w), lambda b, i: (b, i, 0))
    pos = pl.BlockSpec((tm, LANES), lambda b, i: (i, 0))
    consts = [lw["pre_mix_g"], lw["w_uv"], lw["w_lat"], lw["w_c"], lw["a_ln_g"], lw["a_ln_b"], lw["b_q_norm_g"],
              lw["b_kv_norm_g"], lw["w_uq"], lw["w_uq_sw"], lw["w_uk"], lw["w_uvv"]]
    widths = [D_MODEL, A_WIDTH, A_WIDTH, B_HEADS * B_QK_PAD, B_HEADS * B_QK_PAD, B_HEADS * B_VDIM,
              C_WIDTH, C_KV_WIDTH, C_KV_WIDTH]
    return pl.pallas_call(
        _proj_kernel,
        grid=grid,
        in_specs=[tok(D_MODEL)] + [_const_spec(a) for a in consts] + [pos] * 4,
        out_specs=[tok(w) for w in widths],
        out_shape=[jax.ShapeDtypeStruct((bsz, seq, w), BF16) for w in widths],
        compiler_params=_params(2),
        name="proj",
    )(x, *consts, *rope)


def _mla_kernel(q_ref, k_ref, v_ref, o_ref, *scratch, tq, tk):
    per_block = len(scratch) // MLA_QBLOCKS

    def group(i, carry):
        for qb in range(MLA_QBLOCKS):
            rows = pl.ds(pl.multiple_of((i * MLA_QBLOCKS + qb) * tq, tq), tq)
            _mla_block(q_ref.at[0, rows, :], k_ref, v_ref, o_ref.at[0, rows, :],
                       *scratch[qb * per_block:(qb + 1) * per_block], tk=tk)
        return carry

    lax.fori_loop(0, q_ref.shape[1] // (tq * MLA_QBLOCKS), group, 0)


def _mla_block(q_ref, k_ref, v_ref, o_ref, m_sc, acc_sc, s0_sc, s1_sc, p0_sc, p1_sc, a0_sc, a1_sc, *, tk):
    n = k_ref.shape[1] // tk
    s_sc = (s0_sc, s1_sc)
    p_sc = (p0_sc, p1_sc)
    a_sc = (a0_sc, a1_sc)
    m_sc[...] = jnp.full(m_sc.shape, -jnp.inf, F32)
    acc_sc[...] = jnp.zeros(acc_sc.shape, F32)
    ones = jnp.ones((tk, LANES), BF16)

    def scores(c, par):
        s_sc[par][...] = _dot_nt(q_ref[...], k_ref[0, c * tk:(c + 1) * tk, :])

    def softmax(par):
        chunks = [s_sc[par][:, c * LANES:(c + 1) * LANES] for c in range(tk // LANES)]
        m_prev = m_sc[...]
        m_new = jnp.maximum(m_prev, jnp.max(functools.reduce(jnp.maximum, chunks), axis=-1, keepdims=True))
        m_sc[...] = m_new
        a_sc[par][...] = jnp.exp2(m_prev - m_new)
        for c, sc in enumerate(chunks):
            p_sc[par][:, c * LANES:(c + 1) * LANES] = jnp.exp2((sc - m_new).astype(BF16))

    def values(c, par):
        v1 = jnp.concatenate([v_ref[0, c * tk:(c + 1) * tk, :], ones], axis=1)
        alpha = a_sc[par][...]
        acc_sc[...] = jnp.concatenate([alpha, alpha], axis=1) * acc_sc[...] + _dot(p_sc[par][...], v1)

    scores(0, 0)
    scores(1, 1)
    softmax(0)
    for t in range(n - 2):
        scores(t + 2, t % 2)
        softmax((t + 1) % 2)
        values(t, t % 2)
    softmax((n - 1) % 2)
    values(n - 2, n % 2)
    values(n - 1, (n - 1) % 2)
    o_ref[...] = (acc_sc[:, :B_VDIM] / acc_sc[:, B_VDIM:]).astype(BF16)


def _mla(qm, km, vm):
    bsz, seq, _ = qm.shape
    tq = min(TQ_MLA, seq // MLA_QBLOCKS)
    tk = min(TK_MLA, seq // 2)
    assert seq % (tq * MLA_QBLOCKS) == 0
    row_stat = pltpu.VMEM((tq, LANES), F32)
    block_scratch = [row_stat, pltpu.VMEM((tq, B_VDIM + LANES), F32),
                     pltpu.VMEM((tq, tk), F32), pltpu.VMEM((tq, tk), F32),
                     pltpu.VMEM((tq, tk), BF16), pltpu.VMEM((tq, tk), BF16), row_stat, row_stat]
    return pl.pallas_call(
        functools.partial(_mla_kernel, tq=tq, tk=tk),
        grid=(bsz, B_HEADS),
        in_specs=[
            pl.BlockSpec((1, seq, B_QK_PAD), lambda b, h: (b, 0, h)),
            pl.BlockSpec((1, seq, B_QK_PAD), lambda b, h: (b, 0, h)),
            pl.BlockSpec((1, seq, B_VDIM), lambda b, h: (b, 0, h)),
        ],
        out_specs=pl.BlockSpec((1, seq, B_VDIM), lambda b, h: (b, 0, h)),
        out_shape=jax.ShapeDtypeStruct((bsz, seq, B_HEADS * B_VDIM), BF16),
        scratch_shapes=block_scratch * MLA_QBLOCKS,
        compiler_params=_params(2),
        name="mla",
    )(qm, km, vm)


def _swa_kernel(q_ref, kc_ref, kp_ref, kn_ref, vc_ref, vp_ref, vn_ref, sink_ref, o_ref, *, nblk):
    i = pl.program_id(1)
    last = pl.num_programs(1) - 1
    rows = C_REP * CHUNK
    qi = lax.broadcasted_iota(jnp.int32, (rows, CHUNK), 0) % CHUNK
    kj = lax.broadcasted_iota(jnp.int32, (rows, CHUNK), 1)
    band_prev = kj >= qi
    band_next = kj <= qi
    neg = jnp.float32(-jnp.inf)

    def blk(ref_c, ref_e, n, g, edge_lo):
        cols = slice(g * C_HD, (g + 1) * C_HD)
        if edge_lo:
            return ref_e[0, :, cols] if n == 0 else ref_c[0, (n - 1) * CHUNK:n * CHUNK, cols]
        return ref_e[0, :, cols] if n == nblk - 1 else ref_c[0, (n + 1) * CHUNK:(n + 2) * CHUNK, cols]

    for n in range(nblk):
        tok = slice(n * CHUNK, (n + 1) * CHUNK)
        for g in range(C_KV_HEADS):
            cols = slice(g * C_HD, (g + 1) * C_HD)
            q4 = jnp.concatenate([q_ref[0, tok, (g * C_REP + r) * C_HD:(g * C_REP + r + 1) * C_HD]
                                  for r in range(C_REP)], axis=0)
            sink = jnp.concatenate([jnp.broadcast_to(sink_ref[g * C_REP + r:g * C_REP + r + 1, :], (CHUNK, LANES))
                                    for r in range(C_REP)], axis=0)
            k3 = jnp.concatenate([blk(kc_ref, kp_ref, n, g, True), kc_ref[0, tok, cols],
                                  blk(kc_ref, kn_ref, n, g, False)], axis=0)
            v3 = jnp.concatenate([blk(vc_ref, vp_ref, n, g, True), vc_ref[0, tok, cols],
                                  blk(vc_ref, vn_ref, n, g, False)], axis=0)
            s = _dot_nt(q4, k3)
            ok_prev = band_prev if n > 0 else jnp.logical_and(band_prev, i > 0)
            ok_next = band_next if n < nblk - 1 else jnp.logical_and(band_next, i < last)
            s0 = jnp.where(ok_prev, s[:, :CHUNK], neg)
            s1 = s[:, CHUNK:2 * CHUNK]
            s2 = jnp.where(ok_next, s[:, 2 * CHUNK:], neg)
            m = jnp.max(jnp.maximum(jnp.maximum(s0, s1), s2), axis=-1, keepdims=True)
            m = jnp.maximum(m, sink)
            es = [jnp.exp2(t - m) for t in (s0, s1, s2)]
            den = jnp.sum(es[0] + es[1] + es[2], axis=-1, keepdims=True) + jnp.exp2(sink - m)
            p = jnp.concatenate([e.astype(BF16) for e in es], axis=1)
            o = _dot(p, v3) / den
            for r in range(C_REP):
                hc = (g * C_REP + r) * C_HD
                o_ref[0, tok, hc:hc + C_HD] = o[r * CHUNK:(r + 1) * CHUNK].astype(BF16)


def _swa(cq, ck, cv, sink):
    bsz, seq, _ = cq.shape
    nblk = min(SWA_BLOCKS, seq // CHUNK)
    ts = nblk * CHUNK
    nb = seq // CHUNK
    cur = lambda w: pl.BlockSpec((1, ts, w), lambda b, i: (b, i, 0))
    prev = pl.BlockSpec((1, CHUNK, C_KV_WIDTH), lambda b, i: (b, jnp.maximum(i * nblk - 1, 0), 0))
    nxt = pl.BlockSpec((1, CHUNK, C_KV_WIDTH), lambda b, i: (b, jnp.minimum((i + 1) * nblk, nb - 1), 0))
    return pl.pallas_call(
        functools.partial(_swa_kernel, nblk=nblk),
        grid=(bsz, seq // ts),
        in_specs=[cur(C_WIDTH), cur(C_KV_WIDTH), prev, nxt, cur(C_KV_WIDTH), prev, nxt, _const_spec(sink)],
        out_specs=cur(C_WIDTH),
        out_shape=jax.ShapeDtypeStruct((bsz, seq, C_WIDTH), BF16),
        compiler_params=_params(2),
        name="swa",
    )(cq, ck, ck, ck, cv, cv, cv, sink)


def _merge_kernel(x_ref, h_ref, u_ref, v_ref, ob_ref, oc_ref, wg_ref, ws_ref, bs_ref, wpa_ref, wpb_ref, wpc_ref,
                  wo_ref, g_ref, out_ref, ya_sc, mg_sc):
    tm = x_ref.shape[1]
    for c in range(tm // CHUNK):
        tok = slice(c * CHUNK, (c + 1) * CHUNK)
        for g in range(A_GROUPS):
            cols = slice(g * LANES, (g + 1) * LANES)
            mixed = _dot(ws_ref[g], v_ref[0, tok, cols]) + bs_ref[g]
            ya_sc[tok, cols] = (u_ref[0, tok, cols].astype(F32) * mixed).astype(BF16)
    h = h_ref[0]
    ya = ya_sc[...]
    ob = ob_ref[0]
    oc = oc_ref[0]
    nc = 2 * LANES
    for c in range(0, D_MODEL, nc):
        cols = slice(c, c + nc)
        acc = jax.nn.sigmoid(_dot(h, wg_ref[:, c:c + nc])) * _dot(ya, wpa_ref[:, cols])
        acc += jax.nn.sigmoid(_dot(h, wg_ref[:, D_MODEL + c:D_MODEL + c + nc])) * _dot(ob, wpb_ref[:, cols])
        acc += jax.nn.sigmoid(_dot(h, wg_ref[:, 2 * D_MODEL + c:2 * D_MODEL + c + nc])) * _dot(oc, wpc_ref[:, cols])
        mg_sc[:, cols] = acc.astype(BF16)
    mix = _dot(mg_sc[...], wo_ref[...])
    out_ref[0] = x_ref[0] + _rms(mix, g_ref[...])


def _merge(x, h, u, v, ob, oc, lw):
    bsz, seq, _ = x.shape
    tm = min(TM_MERGE, seq)
    tok = pl.BlockSpec((1, tm, D_MODEL), lambda b, i: (b, i, 0))
    consts = [lw["w_gates"], lw["a_w_s"], lw["a_b_s"], lw["w_pa"], lw["w_pb"], lw["w_pc"], lw["w_o"],
              lw["post_mix_g"]]
    return pl.pallas_call(
        _merge_kernel,
        grid=(bsz, seq // tm),
        in_specs=[tok] * 6 + [_const_spec(a) for a in consts],
        out_specs=tok,
        out_shape=jax.ShapeDtypeStruct(x.shape, F32),
        scratch_shapes=[pltpu.VMEM((tm, A_WIDTH), BF16), pltpu.VMEM((tm, D_MODEL), BF16)],
        compiler_params=_params(2),
        name="merge",
    )(x, h, u, v, ob, oc, *consts)


def _ffn_kernel(x_ref, gpre_ref, wg_ref, wu_ref, wd_ref, gpost_ref, out_ref, a_sc):
    x = x_ref[0]
    h = _rms(x, gpre_ref[...]).astype(BF16)
    nc = 2 * LANES
    for c in range(0, D_FF, nc):
        gt = _dot(h, wg_ref[:, c:c + nc])
        a_sc[:, c:c + nc] = (gt * jax.nn.sigmoid(gt) * _dot(h, wu_ref[:, c:c + nc])).astype(BF16)
    f = _dot(a_sc[...], wd_ref[...])
    out_ref[0] = x + _rms(f, gpost_ref[...])


def _ffn(x, lw):
    bsz, seq, _ = x.shape
    tm = min(TM_FFN, seq)
    tok = pl.BlockSpec((1, tm, D_MODEL), lambda b, i: (b, i, 0))
    consts = [lw["pre_ffn_g"], lw["w_gate"], lw["w_up"], lw["w_down"], lw["post_ffn_g"]]
    return pl.pallas_call(
        _ffn_kernel,
        grid=(bsz, seq // tm),
        in_specs=[tok] + [_const_spec(a) for a in consts],
        out_specs=tok,
        out_shape=jax.ShapeDtypeStruct(x.shape, F32),
        scratch_shapes=[pltpu.VMEM((tm, D_FF), BF16)],
        compiler_params=_params(2),
        name="ffn",
    )(x, *consts)


def _rope_tables(seq):
    pos = jnp.arange(seq, dtype=F32)[:, None]

    def tables(dim):
        inv = 1.0 / (ROPE_THETA ** (jnp.arange(0, dim, 2, dtype=F32) / dim))
        ang = pos * inv[None, :]
        return jnp.cos(ang), jnp.sin(ang)

    cb, sb = tables(B_ROPE)
    zb = jnp.zeros((seq, LANES - B_ROPE), F32)
    cc, sc = tables(C_HD)
    return (jnp.concatenate([cb, cb, zb], axis=1), jnp.concatenate([-sb, sb, zb], axis=1),
            jnp.concatenate([cc, cc], axis=1), jnp.concatenate([-sc, sc], axis=1))


def _prep_layer(l, p):
    row = lambda a: a[l][None, :].astype(F32)
    w_in = p["w_in"][l]
    offs = [0]
    for s in IN_SIZES:
        offs.append(offs[-1] + s)
    seg = lambda k: w_in[:, offs[k]:offs[k + 1]]
    w_kr = seg(4)
    half = B_ROPE // 2
    zpad = jnp.zeros((D_MODEL, LANES - B_ROPE), F32)
    w_lat = jnp.concatenate([seg(2), seg(3), w_kr, zpad, w_kr[:, half:], w_kr[:, :half], zpad], axis=1)
    w_uq = p["b_w_uq"][l].reshape(B_Q_RANK, B_HEADS, B_NOPE + B_ROPE)
    q_pe = w_uq[:, :, B_NOPE:]
    zq = jnp.zeros((B_Q_RANK, B_HEADS, LANES - B_ROPE), F32)
    w_uq_main = jnp.concatenate([w_uq[:, :, :B_NOPE], q_pe, zq], axis=2).reshape(B_Q_RANK, B_HEADS * B_QK_PAD)
    w_uq_sw = jnp.concatenate([q_pe[:, :, half:], q_pe[:, :, :half], zq], axis=2).reshape(B_Q_RANK, B_HEADS * LANES)
    w_ukv = p["b_w_ukv"][l].reshape(B_KV_RANK, B_HEADS, B_NOPE + B_VDIM)
    return {
        "pre_mix_g": row(p["pre_mix_g"]),
        "w_uv": jnp.concatenate([seg(0), seg(1)], axis=1).astype(BF16),
        "w_lat": w_lat.astype(BF16),
        "w_c": jnp.concatenate([seg(5), seg(6), seg(7)], axis=1).astype(BF16),
        "w_gates": seg(8).astype(BF16),
        "a_ln_g": row(p["a_ln_g"]),
        "a_ln_b": row(p["a_ln_b"]),
        "b_q_norm_g": row(p["b_q_norm_g"]),
        "b_kv_norm_g": row(p["b_kv_norm_g"]),
        "w_uq": w_uq_main.astype(BF16),
        "w_uq_sw": w_uq_sw.astype(BF16),
        "w_uk": w_ukv[:, :, :B_NOPE].reshape(B_KV_RANK, B_HEADS * B_NOPE).astype(BF16),
        "w_uvv": w_ukv[:, :, B_NOPE:].reshape(B_KV_RANK, B_HEADS * B_VDIM).astype(BF16),
        "a_w_s": p["a_w_s"][l].astype(BF16),
        "a_b_s": jnp.broadcast_to(p["a_b_s"][l][:, :, None], (A_GROUPS, CHUNK, LANES)).astype(F32),
        "c_sink": jnp.broadcast_to((p["c_sink"][l] * LOG2E)[:, None], (C_HEADS, LANES)).astype(F32),
        "w_pa": p["w_pa"][l].astype(BF16),
        "w_pb": p["w_pb"][l].astype(BF16),
        "w_pc": p["w_pc"][l].astype(BF16),
        "w_o": p["w_o"][l].astype(BF16),
        "post_mix_g": row(p["post_mix_g"]),
        "pre_ffn_g": row(p["pre_ffn_g"]),
        "w_gate": p["w_gate"][l].astype(BF16),
        "w_up": p["w_up"][l].astype(BF16),
        "w_down": p["w_down"][l].astype(BF16),
        "post_ffn_g": row(p["post_ffn_g"]),
    }


def _trunk(x, layers):
    rope = _rope_tables(x.shape[1])
    for lw in layers:
        h, u, v, qm, km, vm, cq, ck, cv = _proj(x, lw, rope)
        ob = _mla(qm, km, vm)
        oc = _swa(cq, ck, cv, lw["c_sink"])
        x = _merge(x, h, u, v, ob, oc, lw)
        x = _ffn(x, lw)
    return x


def kernel(x_prompt, x_sample, pre_mix_g, w_in, a_ln_g, a_ln_b, a_w_s, a_b_s, b_q_norm_g, b_w_uq, b_kv_norm_g,
           b_w_ukv, c_sink, w_pa, w_pb, w_pc, w_o, post_mix_g, pre_ffn_g, w_gate, w_up, w_down, post_ffn_g):
    p = dict(pre_mix_g=pre_mix_g, w_in=w_in, a_ln_g=a_ln_g, a_ln_b=a_ln_b, a_w_s=a_w_s, a_b_s=a_b_s,
             b_q_norm_g=b_q_norm_g, b_w_uq=b_w_uq, b_kv_norm_g=b_kv_norm_g, b_w_ukv=b_w_ukv, c_sink=c_sink,
             w_pa=w_pa, w_pb=w_pb, w_pc=w_pc, w_o=w_o, post_mix_g=post_mix_g, pre_ffn_g=pre_ffn_g,
             w_gate=w_gate, w_up=w_up, w_down=w_down, post_ffn_g=post_ffn_g)
    layers = [_prep_layer(l, p) for l in range(w_in.shape[0])]
    return (_trunk(x_prompt, layers), _trunk(x_sample, layers))
```

```python
import functools
import math

import jax
import jax.numpy as jnp
from jax import lax
from jax.experimental import pallas as pl
from jax.experimental.pallas import tpu as pltpu

F32 = jnp.float32
BF16 = jnp.bfloat16

D_MODEL = 1024
EPS = 1e-6
ROPE_THETA = 10000.0
N_BRANCH = 3
CHUNK = 128
A_GROUPS = 8
A_WIDTH = 1024
B_HEADS = 8
B_Q_RANK = 384
B_KV_RANK = 256
B_NOPE = 128
B_ROPE = 64
B_VDIM = 128
C_HEADS = 8
C_KV_HEADS = 2
C_HD = 128
C_WIDTH = C_HEADS * C_HD
C_KV_WIDTH = C_KV_HEADS * C_HD
C_REP = C_HEADS // C_KV_HEADS
WINDOW = 128
D_FF = -(-8 * D_MODEL // (3 * 256)) * 256
IN_SIZES = (A_WIDTH, A_WIDTH, B_Q_RANK, B_KV_RANK, B_ROPE, C_WIDTH, C_KV_WIDTH, C_KV_WIDTH, N_BRANCH * D_MODEL)

LANES = 128
B_QK_PAD = 2 * LANES
LOG2E = math.log2(math.e)
B_QSCALE = (B_NOPE + B_ROPE) ** -0.5 * LOG2E
C_QSCALE = C_HD ** -0.5 * LOG2E

V7X_VMEM_BYTES = 64 * 1024 * 1024
VMEM_LIMIT = V7X_VMEM_BYTES - 8 * 1024 * 1024

TM_PROJ = 512
TM_MERGE = 512
TM_FFN = 512
TQ_MLA = 512
TK_MLA = 1024
MLA_QBLOCKS = 2
SWA_BLOCKS = 4


def _dot(a, b):
    return jnp.dot(a, b, preferred_element_type=F32)


def _dot_nt(a, b):
    return lax.dot_general(a, b, (((1,), (1,)), ((), ())), preferred_element_type=F32)


def _rms(x, g):
    return x * lax.rsqrt(jnp.mean(x * x, axis=-1, keepdims=True) + EPS) * g


def _const_spec(arr):
    nd = arr.ndim
    return pl.BlockSpec(arr.shape, lambda *_: (0,) * nd, pipeline_mode=pl.Buffered(1))


def _params(n_axes, flags=None):
    return pltpu.CompilerParams(dimension_semantics=("arbitrary",) * n_axes, vmem_limit_bytes=VMEM_LIMIT,
                                flags=flags)


def _proj_kernel(x_ref, g_ref, wuv_ref, wlat_ref, wc_ref, lng_ref, lnb_ref, qg_ref, kvg_ref,
                 wuq_ref, wuqs_ref, wuk_ref, wuvv_ref, cb_ref, sb_ref, cc_ref, sc_ref,
                 h_ref, u_ref, v_ref, qm_ref, km_ref, vm_ref, cq_ref, ck_ref, cv_ref):
    h = _rms(x_ref[0], g_ref[...]).astype(BF16)
    h_ref[0] = h

    nc = 2 * LANES
    for c in range(0, A_WIDTH, nc):
        u_ref[0, :, c:c + nc] = jax.nn.gelu(_dot(h, wuv_ref[:, c:c + nc])).astype(BF16)
    v = jax.nn.gelu(_dot(h, wuv_ref[:, A_WIDTH:]))
    vc = v - jnp.mean(v, axis=-1, keepdims=True)
    var = jnp.mean(vc * vc, axis=-1, keepdims=True)
    v_ref[0] = (vc * lax.rsqrt(var + EPS) * lng_ref[...] + lnb_ref[...]).astype(BF16)

    lat = _dot(h, wlat_ref[...])
    o1 = B_Q_RANK
    o2 = o1 + B_KV_RANK
    cqn = _rms(lat[:, :o1], qg_ref[...]).astype(BF16)
    ckvn = _rms(lat[:, o1:o2], kvg_ref[...]).astype(BF16)
    cb = cb_ref[...]
    sb = sb_ref[...]
    kpe = (lat[:, o2:o2 + LANES] * cb + lat[:, o2 + LANES:o2 + 2 * LANES] * sb).astype(BF16)
    for hp in range(0, B_HEADS, 2):
        qs2 = _dot(cqn, wuqs_ref[:, hp * LANES:(hp + 2) * LANES])
        for j in range(2):
            c0 = (hp + j) * B_QK_PAD
            qh = _dot(cqn, wuq_ref[:, c0:c0 + B_QK_PAD])
            qs = qs2[:, j * LANES:(j + 1) * LANES]
            qm_ref[0, :, c0:c0 + LANES] = (qh[:, :LANES] * B_QSCALE).astype(BF16)
            qm_ref[0, :, c0 + LANES:c0 + B_QK_PAD] = ((qh[:, LANES:] * cb + qs * sb) * B_QSCALE).astype(BF16)
            km_ref[0, :, c0 + LANES:c0 + B_QK_PAD] = kpe
        kn = _dot(ckvn, wuk_ref[:, hp * LANES:(hp + 2) * LANES])
        km_ref[0, :, hp * B_QK_PAD:hp * B_QK_PAD + LANES] = kn[:, :LANES].astype(BF16)
        km_ref[0, :, (hp + 1) * B_QK_PAD:(hp + 1) * B_QK_PAD + LANES] = kn[:, LANES:].astype(BF16)
        vm_ref[0, :, hp * LANES:(hp + 2) * LANES] = _dot(ckvn, wuvv_ref[:, hp * LANES:(hp + 2) * LANES]).astype(BF16)

    cc = cc_ref[...]
    sc = sc_ref[...]
    for c in range(0, C_WIDTH + C_KV_WIDTH, nc):
        qk = _dot(h, wc_ref[:, c:c + nc])
        for j in range(nc // LANES):
            t = qk[:, j * LANES:(j + 1) * LANES]
            r = t * cc + pltpu.roll(t, C_HD // 2, 1) * sc
            col = c + j * LANES
            if col < C_WIDTH:
                cq_ref[0, :, col:col + LANES] = (r * C_QSCALE).astype(BF16)
            else:
                ck_ref[0, :, col - C_WIDTH:col - C_WIDTH + LANES] = r.astype(BF16)
    cv_ref[0] = _dot(h, wc_ref[:, C_WIDTH + C_KV_WIDTH:]).astype(BF16)


def _proj(x, lw, rope):
    bsz, seq, _ = x.shape
    tm = min(TM_PROJ, seq)
    grid = (bsz, seq // tm)
    tok = lambda w: pl.BlockSpec((1, tm, w), lambda b, i: (b, i, 0))
    pos = pl.BlockSpec((tm, LANES), lambda b, i: (i, 0))
    consts = [lw["pre_mix_g"], lw["w_uv"], lw["w_lat"], lw["w_c"], lw["a_ln_g"], lw["a_ln_b"], lw["b_q_norm_g"],
              lw["b_kv_norm_g"], lw["w_uq"], lw["w_uq_sw"], lw["w_uk"], lw["w_uvv"]]
    widths = [D_MODEL, A_WIDTH, A_WIDTH, B_HEADS * B_QK_PAD, B_HEADS * B_QK_PAD, B_HEADS * B_VDIM,
              C_WIDTH, C_KV_WIDTH, C_KV_WIDTH]
    return pl.pallas_call(
        _proj_kernel,
        grid=grid,
        in_specs=[tok(D_MODEL)] + [_const_spec(a) for a in consts] + [pos] * 4,
        out_specs=[tok(w) for w in widths],
        out_shape=[jax.ShapeDtypeStruct((bsz, seq, w), BF16) for w in widths],
        compiler_params=_params(2),
        name="proj",
    )(x, *consts, *rope)


def _mla_kernel(q_ref, k_ref, v_ref, o_ref, *scratch, tq, tk):
    per_block = len(scratch) // MLA_QBLOCKS

    def group(i, carry):
        for qb in range(MLA_QBLOCKS):
            rows = pl.ds(pl.multiple_of((i * MLA_QBLOCKS + qb) * tq, tq), tq)
            _mla_block(q_ref.at[0, rows, :], k_ref, v_ref, o_ref.at[0, rows, :],
                       *scratch[qb * per_block:(qb + 1) * per_block], tk=tk)
        return carry

    lax.fori_loop(0, q_ref.shape[1] // (tq * MLA_QBLOCKS), group, 0)


def _mla_block(q_ref, k_ref, v_ref, o_ref, m_sc, acc_sc, s0_sc, s1_sc, p0_sc, p1_sc, a0_sc, a1_sc, *, tk):
    n = k_ref.shape[1] // tk
    s_sc = (s0_sc, s1_sc)
    p_sc = (p0_sc, p1_sc)
    a_sc = (a0_sc, a1_sc)
    m_sc[...] = jnp.full(m_sc.shape, -jnp.inf, F32)
    acc_sc[...] = jnp.zeros(acc_sc.shape, F32)
    ones = jnp.ones((tk, LANES), BF16)

    def scores(c, par):
        s_sc[par][...] = _dot_nt(q_ref[...], k_ref[0, c * tk:(c + 1) * tk, :])

    def softmax(par):
        chunks = [s_sc[par][:, c * LANES:(c + 1) * LANES] for c in range(tk // LANES)]
        m_prev = m_sc[...]
        m_new = jnp.maximum(m_prev, jnp.max(functools.reduce(jnp.maximum, chunks), axis=-1, keepdims=True))
        m_sc[...] = m_new
        a_sc[par][...] = jnp.exp2(m_prev - m_new)
        for c, sc in enumerate(chunks):
            p_sc[par][:, c * LANES:(c + 1) * LANES] = jnp.exp2((sc - m_new).astype(BF16))

    def values(c, par):
        v1 = jnp.concatenate([v_ref[0, c * tk:(c + 1) * tk, :], ones], axis=1)
        alpha = a_sc[par][...]
        acc_sc[...] = jnp.concatenate([alpha, alpha], axis=1) * acc_sc[...] + _dot(p_sc[par][...], v1)

    scores(0, 0)
    for t in range(n - 1):
        scores(t + 1, (t + 1) % 2)
        softmax(t % 2)
        values(t, t % 2)
    softmax((n - 1) % 2)
    values(n - 1, (n - 1) % 2)
    o_ref[...] = (acc_sc[:, :B_VDIM] / acc_sc[:, B_VDIM:]).astype(BF16)


def _mla(qm, km, vm):
    bsz, seq, _ = qm.shape
    tq = min(TQ_MLA, seq // MLA_QBLOCKS)
    tk = min(TK_MLA, seq // 2)
    assert seq % (tq * MLA_QBLOCKS) == 0
    row_stat = pltpu.VMEM((tq, LANES), F32)
    block_scratch = [row_stat, pltpu.VMEM((tq, B_VDIM + LANES), F32),
                     pltpu.VMEM((tq, tk), F32), pltpu.VMEM((tq, tk), F32),
                     pltpu.VMEM((tq, tk), BF16), pltpu.VMEM((tq, tk), BF16), row_stat, row_stat]
    return pl.pallas_call(
        functools.partial(_mla_kernel, tq=tq, tk=tk),
        grid=(bsz, B_HEADS),
        in_specs=[
            pl.BlockSpec((1, seq, B_QK_PAD), lambda b, h: (b, 0, h)),
            pl.BlockSpec((1, seq, B_QK_PAD), lambda b, h: (b, 0, h)),
            pl.BlockSpec((1, seq, B_VDIM), lambda b, h: (b, 0, h)),
        ],
        out_specs=pl.BlockSpec((1, seq, B_VDIM), lambda b, h: (b, 0, h)),
        out_shape=jax.ShapeDtypeStruct((bsz, seq, B_HEADS * B_VDIM), BF16),
        scratch_shapes=block_scratch * MLA_QBLOCKS,
        compiler_params=_params(2),
        name="mla",
    )(qm, km, vm)


def _swa_kernel(q_ref, kc_ref, kp_ref, kn_ref, vc_ref, vp_ref, vn_ref, sink_ref, o_ref, *, nblk):
    i = pl.program_id(1)
    last = pl.num_programs(1) - 1
    rows = C_REP * CHUNK
    qi = lax.broadcasted_iota(jnp.int32, (rows, CHUNK), 0) % CHUNK
    kj = lax.broadcasted_iota(jnp.int32, (rows, CHUNK), 1)
    band_prev = kj >= qi
    band_next = kj <= qi
    neg = jnp.float32(-jnp.inf)

    def blk(ref_c, ref_e, n, g, edge_lo):
        cols = slice(g * C_HD, (g + 1) * C_HD)
        if edge_lo:
            return ref_e[0, :, cols] if n == 0 else ref_c[0, (n - 1) * CHUNK:n * CHUNK, cols]
        return ref_e[0, :, cols] if n == nblk - 1 else ref_c[0, (n + 1) * CHUNK:(n + 2) * CHUNK, cols]

    for n in range(nblk):
        tok = slice(n * CHUNK, (n + 1) * CHUNK)
        for g in range(C_KV_HEADS):
            cols = slice(g * C_HD, (g + 1) * C_HD)
            q4 = jnp.concatenate([q_ref[0, tok, (g * C_REP + r) * C_HD:(g * C_REP + r + 1) * C_HD]
                                  for r in range(C_REP)], axis=0)
            sink = jnp.concatenate([jnp.broadcast_to(sink_ref[g * C_REP + r:g * C_REP + r + 1, :], (CHUNK, LANES))
                                    for r in range(C_REP)], axis=0)
            k3 = jnp.concatenate([blk(kc_ref, kp_ref, n, g, True), kc_ref[0, tok, cols],
                                  blk(kc_ref, kn_ref, n, g, False)], axis=0)
            v3 = jnp.concatenate([blk(vc_ref, vp_ref, n, g, True), vc_ref[0, tok, cols],
                                  blk(vc_ref, vn_ref, n, g, False)], axis=0)
            s = _dot_nt(q4, k3)
            ok_prev = band_prev if n > 0 else jnp.logical_and(band_prev, i > 0)
            ok_next = band_next if n < nblk - 1 else jnp.logical_and(band_next, i < last)
            s0 = jnp.where(ok_prev, s[:, :CHUNK], neg)
            s1 = s[:, CHUNK:2 * CHUNK]
            s2 = jnp.where(ok_next, s[:, 2 * CHUNK:], neg)
            m = jnp.max(jnp.maximum(jnp.maximum(s0, s1), s2), axis=-1, keepdims=True)
            m = jnp.maximum(m, sink)
            es = [jnp.exp2(t - m) for t in (s0, s1, s2)]
            den = jnp.sum(es[0] + es[1] + es[2], axis=-1, keepdims=True) + jnp.exp2(sink - m)
            p = jnp.concatenate([e.astype(BF16) for e in es], axis=1)
            o = _dot(p, v3) / den
            for r in range(C_REP):
                hc = (g * C_REP + r) * C_HD
                o_ref[0, tok, hc:hc + C_HD] = o[r * CHUNK:(r + 1) * CHUNK].astype(BF16)


def _swa(cq, ck, cv, sink):
    bsz, seq, _ = cq.shape
    nblk = min(SWA_BLOCKS, seq // CHUNK)
    ts = nblk * CHUNK
    nb = seq // CHUNK
    cur = lambda w: pl.BlockSpec((1, ts, w), lambda b, i: (b, i, 0))
    prev = pl.BlockSpec((1, CHUNK, C_KV_WIDTH), lambda b, i: (b, jnp.maximum(i * nblk - 1, 0), 0))
    nxt = pl.BlockSpec((1, CHUNK, C_KV_WIDTH), lambda b, i: (b, jnp.minimum((i + 1) * nblk, nb - 1), 0))
    return pl.pallas_call(
        functools.partial(_swa_kernel, nblk=nblk),
        grid=(bsz, seq // ts),
        in_specs=[cur(C_WIDTH), cur(C_KV_WIDTH), prev, nxt, cur(C_KV_WIDTH), prev, nxt, _const_spec(sink)],
        out_specs=cur(C_WIDTH),
        out_shape=jax.ShapeDtypeStruct((bsz, seq, C_WIDTH), BF16),
        compiler_params=_params(2),
        name="swa",
    )(cq, ck, ck, ck, cv, cv, cv, sink)


def _merge_kernel(x_ref, h_ref, u_ref, v_ref, ob_ref, oc_ref, wg_ref, ws_ref, bs_ref, wpa_ref, wpb_ref, wpc_ref,
                  wo_ref, g_ref, out_ref, ya_sc, mg_sc):
    tm = x_ref.shape[1]
    for c in range(tm // CHUNK):
        tok = slice(c * CHUNK, (c + 1) * CHUNK)
        for g in range(A_GROUPS):
            cols = slice(g * LANES, (g + 1) * LANES)
            mixed = _dot(ws_ref[g], v_ref[0, tok, cols]) + bs_ref[g]
            ya_sc[tok, cols] = (u_ref[0, tok, cols].astype(F32) * mixed).astype(BF16)
    h = h_ref[0]
    ya = ya_sc[...]
    ob = ob_ref[0]
    oc = oc_ref[0]
    nc = 2 * LANES
    for c in range(0, D_MODEL, nc):
        cols = slice(c, c + nc)
        acc = jax.nn.sigmoid(_dot(h, wg_ref[:, c:c + nc])) * _dot(ya, wpa_ref[:, cols])
        acc += jax.nn.sigmoid(_dot(h, wg_ref[:, D_MODEL + c:D_MODEL + c + nc])) * _dot(ob, wpb_ref[:, cols])
        acc += jax.nn.sigmoid(_dot(h, wg_ref[:, 2 * D_MODEL + c:2 * D_MODEL + c + nc])) * _dot(oc, wpc_ref[:, cols])
        mg_sc[:, cols] = acc.astype(BF16)
    mix = _dot(mg_sc[...], wo_ref[...])
    out_ref[0] = x_ref[0] + _rms(mix, g_ref[...])


def _merge(x, h, u, v, ob, oc, lw):
    bsz, seq, _ = x.shape
    tm = min(TM_MERGE, seq)
    tok = pl.BlockSpec((1, tm, D_MODEL), lambda b, i: (b, i, 0))
    consts = [lw["w_gates"], lw["a_w_s"], lw["a_b_s"], lw["w_pa"], lw["w_pb"], lw["w_pc"], lw["w_o"],
              lw["post_mix_g"]]
    return pl.pallas_call(
        _merge_kernel,
        grid=(bsz, seq // tm),
        in_specs=[tok] * 6 + [_const_spec(a) for a in consts],
        out_specs=tok,
        out_shape=jax.ShapeDtypeStruct(x.shape, F32),
        scratch_shapes=[pltpu.VMEM((tm, A_WIDTH), BF16), pltpu.VMEM((tm, D_MODEL), BF16)],
        compiler_params=_params(2),
        name="merge",
    )(x, h, u, v, ob, oc, *consts)


def _ffn_kernel(x_ref, gpre_ref, wg_ref, wu_ref, wd_ref, gpost_ref, out_ref, a_sc):
    x = x_ref[0]
    h = _rms(x, gpre_ref[...]).astype(BF16)
    nc = 2 * LANES
    for c in range(0, D_FF, nc):
        gt = _dot(h, wg_ref[:, c:c + nc])
        a_sc[:, c:c + nc] = (gt * jax.nn.sigmoid(gt) * _dot(h, wu_ref[:, c:c + nc])).astype(BF16)
    f = _dot(a_sc[...], wd_ref[...])
    out_ref[0] = x + _rms(f, gpost_ref[...])


def _ffn(x, lw):
    bsz, seq, _ = x.shape
    tm = min(TM_FFN, seq)
    tok = pl.BlockSpec((1, tm, D_MODEL), lambda b, i: (b, i, 0))
    consts = [lw["pre_ffn_g"], lw["w_gate"], lw["w_up"], lw["w_down"], lw["post_ffn_g"]]
    return pl.pallas_call(
        _ffn_kernel,
        grid=(bsz, seq // tm),
        in_specs=[tok] + [_const_spec(a) for a in consts],
        out_specs=tok,
        out_shape=jax.ShapeDtypeStruct(x.shape, F32),
        scratch_shapes=[pltpu.VMEM((tm, D_FF), BF16)],
        compiler_params=_params(2),
        name="ffn",
    )(x, *consts)


def _rope_tables(seq):
    pos = jnp.arange(seq, dtype=F32)[:, None]

    def tables(dim):
        inv = 1.0 / (ROPE_THETA ** (jnp.arange(0, dim, 2, dtype=F32) / dim))
        ang = pos * inv[None, :]
        return jnp.cos(ang), jnp.sin(ang)

    cb, sb = tables(B_ROPE)
    zb = jnp.zeros((seq, LANES - B_ROPE), F32)
    cc, sc = tables(C_HD)
    return (jnp.concatenate([cb, cb, zb], axis=1), jnp.concatenate([-sb, sb, zb], axis=1),
            jnp.concatenate([cc, cc], axis=1), jnp.concatenate([-sc, sc], axis=1))


def _prep_layer(l, p):
    row = lambda a: a[l][None, :].astype(F32)
    w_in = p["w_in"][l]
    offs = [0]
    for s in IN_SIZES:
        offs.append(offs[-1] + s)
    seg = lambda k: w_in[:, offs[k]:offs[k + 1]]
    w_kr = seg(4)
    half = B_ROPE // 2
    zpad = jnp.zeros((D_MODEL, LANES - B_ROPE), F32)
    w_lat = jnp.concatenate([seg(2), seg(3), w_kr, zpad, w_kr[:, half:], w_kr[:, :half], zpad], axis=1)
    w_uq = p["b_w_uq"][l].reshape(B_Q_RANK, B_HEADS, B_NOPE + B_ROPE)
    q_pe = w_uq[:, :, B_NOPE:]
    zq = jnp.zeros((B_Q_RANK, B_HEADS, LANES - B_ROPE), F32)
    w_uq_main = jnp.concatenate([w_uq[:, :, :B_NOPE], q_pe, zq], axis=2).reshape(B_Q_RANK, B_HEADS * B_QK_PAD)
    w_uq_sw = jnp.concatenate([q_pe[:, :, half:], q_pe[:, :, :half], zq], axis=2).reshape(B_Q_RANK, B_HEADS * LANES)
    w_ukv = p["b_w_ukv"][l].reshape(B_KV_RANK, B_HEADS, B_NOPE + B_VDIM)
    return {
        "pre_mix_g": row(p["pre_mix_g"]),
        "w_uv": jnp.concatenate([seg(0), seg(1)], axis=1).astype(BF16),
        "w_lat": w_lat.astype(BF16),
        "w_c": jnp.concatenate([seg(5), seg(6), seg(7)], axis=1).astype(BF16),
        "w_gates": seg(8).astype(BF16),
        "a_ln_g": row(p["a_ln_g"]),
        "a_ln_b": row(p["a_ln_b"]),
        "b_q_norm_g": row(p["b_q_norm_g"]),
        "b_kv_norm_g": row(p["b_kv_norm_g"]),
        "w_uq": w_uq_main.astype(BF16),
        "w_uq_sw": w_uq_sw.astype(BF16),
        "w_uk": w_ukv[:, :, :B_NOPE].reshape(B_KV_RANK, B_HEADS * B_NOPE).astype(BF16),
        "w_uvv": w_ukv[:, :, B_NOPE:].reshape(B_KV_RANK, B_HEADS * B_VDIM).astype(BF16),
        "a_w_s": p["a_w_s"][l].astype(BF16),
        "a_b_s": jnp.broadcast_to(p["a_b_s"][l][:, :, None], (A_GROUPS, CHUNK, LANES)).astype(F32),
        "c_sink": jnp.broadcast_to((p["c_sink"][l] * LOG2E)[:, None], (C_HEADS, LANES)).astype(F32),
        "w_pa": p["w_pa"][l].astype(BF16),
        "w_pb": p["w_pb"][l].astype(BF16),
        "w_pc": p["w_pc"][l].astype(BF16),
        "w_o": p["w_o"][l].astype(BF16),
        "post_mix_g": row(p["post_mix_g"]),
        "pre_ffn_g": row(p["pre_ffn_g"]),
        "w_gate": p["w_gate"][l].astype(BF16),
        "w_up": p["w_up"][l].astype(BF16),
        "w_down": p["w_down"][l].astype(BF16),
        "post_ffn_g": row(p["post_ffn_g"]),
    }


def _trunk(x, layers):
    rope = _rope_tables(x.shape[1])
    for lw in layers:
        h, u, v, qm, km, vm, cq, ck, cv = _proj(x, lw, rope)
        ob = _mla(qm, km, vm)
        oc = _swa(cq, ck, cv, lw["c_sink"])
        x = _merge(x, h, u, v, ob, oc, lw)
        x = _ffn(x, lw)
    return x


def kernel(x_prompt, x_sample, pre_mix_g, w_in, a_ln_g, a_ln_b, a_w_s, a_b_s, b_q_norm_g, b_w_uq, b_kv_norm_g,
           b_w_ukv, c_sink, w_pa, w_pb, w_pc, w_o, post_mix_g, pre_ffn_g, w_gate, w_up, w_down, post_ffn_g):
    p = dict(pre_mix_g=pre_mix_g, w_in=w_in, a_ln_g=a_ln_g, a_ln_b=a_ln_b, a_w_s=a_w_s, a_b_s=a_b_s,
             b_q_norm_g=b_q_norm_g, b_w_uq=b_w_uq, b_kv_norm_g=b_kv_norm_g, b_w_ukv=b_w_ukv, c_sink=c_sink,
             w_pa=w_pa, w_pb=w_pb, w_pc=w_pc, w_o=w_o, post_mix_g=post_mix_g, pre_ffn_g=pre_ffn_g,
             w_gate=w_gate, w_up=w_up, w_down=w_down, post_ffn_g=post_ffn_g)
    layers = [_prep_layer(l, p) for l in range(w_in.shape[0])]
    return (_trunk(x_prompt, layers), _trunk(x_sample, layers))
```

```python
import functools
import math

import jax
import jax.numpy as jnp
from jax import lax
from jax.experimental import pallas as pl
from jax.experimental.pallas import tpu as pltpu

F32 = jnp.float32
BF16 = jnp.bfloat16

D_MODEL = 1024
EPS = 1e-6
ROPE_THETA = 10000.0
N_BRANCH = 3
CHUNK = 128
A_GROUPS = 8
A_WIDTH = 1024
B_HEADS = 8
B_Q_RANK = 384
B_KV_RANK = 256
B_NOPE = 128
B_ROPE = 64
B_VDIM = 128
C_HEADS = 8
C_KV_HEADS = 2
C_HD = 128
C_WIDTH = C_HEADS * C_HD
C_KV_WIDTH = C_KV_HEADS * C_HD
C_REP = C_HEADS // C_KV_HEADS
WINDOW = 128
D_FF = -(-8 * D_MODEL // (3 * 256)) * 256
IN_SIZES = (A_WIDTH, A_WIDTH, B_Q_RANK, B_KV_RANK, B_ROPE, C_WIDTH, C_KV_WIDTH, C_KV_WIDTH, N_BRANCH * D_MODEL)

LANES = 128
B_QK_PAD = 2 * LANES
LOG2E = math.log2(math.e)
B_QSCALE = (B_NOPE + B_ROPE) ** -0.5 * LOG2E
C_QSCALE = C_HD ** -0.5 * LOG2E

V7X_VMEM_BYTES = 64 * 1024 * 1024
VMEM_LIMIT = V7X_VMEM_BYTES - 8 * 1024 * 1024

TM_PROJ = 512
TM_MERGE = 512
TM_FFN = 512
TQ_MLA = 512
TK_MLA = 2048
MLA_QBLOCKS = 2
MLA_ROW_BLOCK = 64
SWA_BLOCKS = 4


def _dot(a, b):
    return jnp.dot(a, b, preferred_element_type=F32)


def _dot_nt(a, b):
    return lax.dot_general(a, b, (((1,), (1,)), ((), ())), preferred_element_type=F32)


def _rms(x, g):
    return x * lax.rsqrt(jnp.mean(x * x, axis=-1, keepdims=True) + EPS) * g


def _const_spec(arr):
    nd = arr.ndim
    return pl.BlockSpec(arr.shape, lambda *_: (0,) * nd, pipeline_mode=pl.Buffered(1))


def _params(n_axes, flags=None):
    return pltpu.CompilerParams(dimension_semantics=("arbitrary",) * n_axes, vmem_limit_bytes=VMEM_LIMIT,
                                flags=flags)


def _proj_kernel(x_ref, g_ref, wuv_ref, wlat_ref, wc_ref, lng_ref, lnb_ref, qg_ref, kvg_ref,
                 wuq_ref, wuqs_ref, wuk_ref, wuvv_ref, cb_ref, sb_ref, cc_ref, sc_ref,
                 h_ref, u_ref, v_ref, qm_ref, km_ref, vm_ref, cq_ref, ck_ref, cv_ref):
    h = _rms(x_ref[0], g_ref[...]).astype(BF16)
    h_ref[0] = h

    nc = 2 * LANES
    for c in range(0, A_WIDTH, nc):
        u_ref[0, :, c:c + nc] = jax.nn.gelu(_dot(h, wuv_ref[:, c:c + nc])).astype(BF16)
    v = jax.nn.gelu(_dot(h, wuv_ref[:, A_WIDTH:]))
    vc = v - jnp.mean(v, axis=-1, keepdims=True)
    var = jnp.mean(vc * vc, axis=-1, keepdims=True)
    v_ref[0] = (vc * lax.rsqrt(var + EPS) * lng_ref[...] + lnb_ref[...]).astype(BF16)

    lat = _dot(h, wlat_ref[...])
    o1 = B_Q_RANK
    o2 = o1 + B_KV_RANK
    cqn = _rms(lat[:, :o1], qg_ref[...]).astype(BF16)
    ckvn = _rms(lat[:, o1:o2], kvg_ref[...]).astype(BF16)
    cb = cb_ref[...]
    sb = sb_ref[...]
    kpe = (lat[:, o2:o2 + LANES] * cb + lat[:, o2 + LANES:o2 + 2 * LANES] * sb).astype(BF16)
    for hp in range(0, B_HEADS, 2):
        qs2 = _dot(cqn, wuqs_ref[:, hp * LANES:(hp + 2) * LANES])
        for j in range(2):
            c0 = (hp + j) * B_QK_PAD
            qh = _dot(cqn, wuq_ref[:, c0:c0 + B_QK_PAD])
            qs = qs2[:, j * LANES:(j + 1) * LANES]
            qm_ref[0, :, c0:c0 + LANES] = (qh[:, :LANES] * B_QSCALE).astype(BF16)
            qm_ref[0, :, c0 + LANES:c0 + B_QK_PAD] = ((qh[:, LANES:] * cb + qs * sb) * B_QSCALE).astype(BF16)
            km_ref[0, :, c0 + LANES:c0 + B_QK_PAD] = kpe
        kn = _dot(ckvn, wuk_ref[:, hp * LANES:(hp + 2) * LANES])
        km_ref[0, :, hp * B_QK_PAD:hp * B_QK_PAD + LANES] = kn[:, :LANES].astype(BF16)
        km_ref[0, :, (hp + 1) * B_QK_PAD:(hp + 1) * B_QK_PAD + LANES] = kn[:, LANES:].astype(BF16)
        vm_ref[0, :, hp * LANES:(hp + 2) * LANES] = _dot(ckvn, wuvv_ref[:, hp * LANES:(hp + 2) * LANES]).astype(BF16)

    cc = cc_ref[...]
    sc = sc_ref[...]
    for c in range(0, C_WIDTH + C_KV_WIDTH, nc):
        qk = _dot(h, wc_ref[:, c:c + nc])
        for j in range(nc // LANES):
            t = qk[:, j * LANES:(j + 1) * LANES]
            r = t * cc + pltpu.roll(t, C_HD // 2, 1) * sc
            col = c + j * LANES
            if col < C_WIDTH:
                cq_ref[0, :, col:col + LANES] = (r * C_QSCALE).astype(BF16)
            else:
                ck_ref[0, :, col - C_WIDTH:col - C_WIDTH + LANES] = r.astype(BF16)
    cv_ref[0] = _dot(h, wc_ref[:, C_WIDTH + C_KV_WIDTH:]).astype(BF16)


def _proj(x, lw, rope):
    bsz, seq, _ = x.shape
    tm = min(TM_PROJ, seq)
    grid = (bsz, seq // tm)
    tok = lambda w: pl.BlockSpec((1, tm, w), lambda b, i: (b, i, 0))
    pos = pl.BlockSpec((tm, LANES), lambda b, i: (i, 0))
    consts = [lw["pre_mix_g"], lw["w_uv"], lw["w_lat"], lw["w_c"], lw["a_ln_g"], lw["a_ln_b"], lw["b_q_norm_g"],
              lw["b_kv_norm_g"], lw["w_uq"], lw["w_uq_sw"], lw["w_uk"], lw["w_uvv"]]
    widths = [D_MODEL, A_WIDTH, A_WIDTH, B_HEADS * B_QK_PAD, B_HEADS * B_QK_PAD, B_HEADS * B_VDIM,
              C_WIDTH, C_KV_WIDTH, C_KV_WIDTH]
    return pl.pallas_call(
        _proj_kernel,
        grid=grid,
        in_specs=[tok(D_MODEL)] + [_const_spec(a) for a in consts] + [pos] * 4,
        out_specs=[tok(w) for w in widths],
        out_shape=[jax.ShapeDtypeStruct((bsz, seq, w), BF16) for w in widths],
        compiler_params=_params(2),
        name="proj",
    )(x, *consts, *rope)


def _mla_kernel(q_ref, k_ref, v_ref, o_ref, *scratch, tq, tk):
    per_block = len(scratch) // MLA_QBLOCKS

    def group(i, carry):
        for qb in range(MLA_QBLOCKS):
            rows = pl.ds(pl.multiple_of((i * MLA_QBLOCKS + qb) * tq, tq), tq)
            _mla_block(q_ref.at[0, rows, :], k_ref, v_ref, o_ref.at[0, rows, :],
                       *scratch[qb * per_block:(qb + 1) * per_block], tk=tk)
        return carry

    lax.fori_loop(0, q_ref.shape[1] // (tq * MLA_QBLOCKS), group, 0)


def _mla_block(q_ref, k_ref, v_ref, o_ref, m_sc, acc_sc, s0_sc, s1_sc, p0_sc, p1_sc, a0_sc, a1_sc, *, tk):
    n = k_ref.shape[1] // tk
    s_sc = (s0_sc, s1_sc)
    p_sc = (p0_sc, p1_sc)
    a_sc = (a0_sc, a1_sc)
    m_sc[...] = jnp.full(m_sc.shape, -jnp.inf, F32)
    acc_sc[...] = jnp.zeros(acc_sc.shape, F32)
    ones = jnp.ones((tk, LANES), BF16)

    def scores(c, par):
        s_sc[par][...] = _dot_nt(q_ref[...], k_ref[0, c * tk:(c + 1) * tk, :])

    def softmax(par):
        for r0 in range(0, q_ref.shape[0], MLA_ROW_BLOCK):
            rows = slice(r0, r0 + MLA_ROW_BLOCK)
            chunks = [s_sc[par][rows, c * LANES:(c + 1) * LANES] for c in range(tk // LANES)]
            m_prev = m_sc[rows, :]
            m_new = jnp.maximum(m_prev, jnp.max(functools.reduce(jnp.maximum, chunks), axis=-1, keepdims=True))
            m_sc[rows, :] = m_new
            a_sc[par][rows, :] = jnp.exp2(m_prev - m_new)
            for c, sc in enumerate(chunks):
                p_sc[par][rows, c * LANES:(c + 1) * LANES] = jnp.exp2((sc - m_new).astype(BF16))

    def values(c, par):
        v1 = jnp.concatenate([v_ref[0, c * tk:(c + 1) * tk, :], ones], axis=1)
        alpha = a_sc[par][...]
        acc_sc[...] = jnp.concatenate([alpha, alpha], axis=1) * acc_sc[...] + _dot(p_sc[par][...], v1)

    scores(0, 0)
    for t in range(n - 1):
        scores(t + 1, (t + 1) % 2)
        softmax(t % 2)
        values(t, t % 2)
    softmax((n - 1) % 2)
    values(n - 1, (n - 1) % 2)
    o_ref[...] = (acc_sc[:, :B_VDIM] / acc_sc[:, B_VDIM:]).astype(BF16)


def _mla(qm, km, vm):
    bsz, seq, _ = qm.shape
    tq = min(TQ_MLA, seq // MLA_QBLOCKS)
    tk = min(TK_MLA, seq // 2)
    assert seq % (tq * MLA_QBLOCKS) == 0
    row_stat = pltpu.VMEM((tq, LANES), F32)
    block_scratch = [row_stat, pltpu.VMEM((tq, B_VDIM + LANES), F32),
                     pltpu.VMEM((tq, tk), F32), pltpu.VMEM((tq, tk), F32),
                     pltpu.VMEM((tq, tk), BF16), pltpu.VMEM((tq, tk), BF16), row_stat, row_stat]
    return pl.pallas_call(
        functools.partial(_mla_kernel, tq=tq, tk=tk),
        grid=(bsz, B_HEADS),
        in_specs=[
            pl.BlockSpec((1, seq, B_QK_PAD), lambda b, h: (b, 0, h)),
            pl.BlockSpec((1, seq, B_QK_PAD), lambda b, h: (b, 0, h)),
            pl.BlockSpec((1, seq, B_VDIM), lambda b, h: (b, 0, h)),
        ],
        out_specs=pl.BlockSpec((1, seq, B_VDIM), lambda b, h: (b, 0, h)),
        out_shape=jax.ShapeDtypeStruct((bsz, seq, B_HEADS * B_VDIM), BF16),
        scratch_shapes=block_scratch * MLA_QBLOCKS,
        compiler_params=_params(2),
        name="mla",
    )(qm, km, vm)


def _swa_kernel(q_ref, kc_ref, kp_ref, kn_ref, vc_ref, vp_ref, vn_ref, sink_ref, o_ref, *, nblk):
    i = pl.program_id(1)
    last = pl.num_programs(1) - 1
    rows = C_REP * CHUNK
    qi = lax.broadcasted_iota(jnp.int32, (rows, CHUNK), 0) % CHUNK
    kj = lax.broadcasted_iota(jnp.int32, (rows, CHUNK), 1)
    band_prev = kj >= qi
    band_next = kj <= qi
    neg = jnp.float32(-jnp.inf)
    ones = jnp.ones((3 * CHUNK, LANES), BF16)

    def blk(ref_c, ref_e, n, g, edge_lo):
        cols = slice(g * C_HD, (g + 1) * C_HD)
        if edge_lo:
            return ref_e[0, :, cols] if n == 0 else ref_c[0, (n - 1) * CHUNK:n * CHUNK, cols]
        return ref_e[0, :, cols] if n == nblk - 1 else ref_c[0, (n + 1) * CHUNK:(n + 2) * CHUNK, cols]

    for n in range(nblk):
        tok = slice(n * CHUNK, (n + 1) * CHUNK)
        for g in range(C_KV_HEADS):
            cols = slice(g * C_HD, (g + 1) * C_HD)
            q4 = jnp.concatenate([q_ref[0, tok, (g * C_REP + r) * C_HD:(g * C_REP + r + 1) * C_HD]
                                  for r in range(C_REP)], axis=0)
            sink = jnp.concatenate([jnp.broadcast_to(sink_ref[g * C_REP + r:g * C_REP + r + 1, :], (CHUNK, LANES))
                                    for r in range(C_REP)], axis=0)
            k3 = jnp.concatenate([blk(kc_ref, kp_ref, n, g, True), kc_ref[0, tok, cols],
                                  blk(kc_ref, kn_ref, n, g, False)], axis=0)
            v3 = jnp.concatenate([blk(vc_ref, vp_ref, n, g, True), vc_ref[0, tok, cols],
                                  blk(vc_ref, vn_ref, n, g, False)], axis=0)
            s = _dot_nt(q4, k3)
            ok_prev = band_prev if n > 0 else jnp.logical_and(band_prev, i > 0)
            ok_next = band_next if n < nblk - 1 else jnp.logical_and(band_next, i < last)
            s0 = jnp.where(ok_prev, s[:, :CHUNK], neg)
            s1 = s[:, CHUNK:2 * CHUNK]
            s2 = jnp.where(ok_next, s[:, 2 * CHUNK:], neg)
            m = jnp.max(jnp.maximum(jnp.maximum(s0, s1), s2), axis=-1, keepdims=True)
            m = jnp.maximum(m, sink)
            p = jnp.concatenate([jnp.exp2((t - m).astype(BF16)) for t in (s0, s1, s2)], axis=1)
            pv = _dot(p, jnp.concatenate([v3, ones], axis=1))
            o = pv[:, :C_HD] / (pv[:, C_HD:] + jnp.exp2(sink - m))
            for r in range(C_REP):
                hc = (g * C_REP + r) * C_HD
                o_ref[0, tok, hc:hc + C_HD] = o[r * CHUNK:(r + 1) * CHUNK].astype(BF16)


def _swa(cq, ck, cv, sink):
    bsz, seq, _ = cq.shape
    nblk = min(SWA_BLOCKS, seq // CHUNK)
    ts = nblk * CHUNK
    nb = seq // CHUNK
    cur = lambda w: pl.BlockSpec((1, ts, w), lambda b, i: (b, i, 0))
    prev = pl.BlockSpec((1, CHUNK, C_KV_WIDTH), lambda b, i: (b, jnp.maximum(i * nblk - 1, 0), 0))
    nxt = pl.BlockSpec((1, CHUNK, C_KV_WIDTH), lambda b, i: (b, jnp.minimum((i + 1) * nblk, nb - 1), 0))
    return pl.pallas_call(
        functools.partial(_swa_kernel, nblk=nblk),
        grid=(bsz, seq // ts),
        in_specs=[cur(C_WIDTH), cur(C_KV_WIDTH), prev, nxt, cur(C_KV_WIDTH), prev, nxt, _const_spec(sink)],
        out_specs=cur(C_WIDTH),
        out_shape=jax.ShapeDtypeStruct((bsz, seq, C_WIDTH), BF16),
        compiler_params=_params(2),
        name="swa",
    )(cq, ck, ck, ck, cv, cv, cv, sink)


def _merge_kernel(x_ref, h_ref, u_ref, v_ref, ob_ref, oc_ref, wg_ref, ws_ref, bs_ref, wpa_ref, wpb_ref, wpc_ref,
                  wo_ref, g_ref, out_ref, ya_sc, mg_sc):
    tm = x_ref.shape[1]
    for c in range(tm // CHUNK):
        tok = slice(c * CHUNK, (c + 1) * CHUNK)
        for g in range(A_GROUPS):
            cols = slice(g * LANES, (g + 1) * LANES)
            mixed = _dot(ws_ref[g], v_ref[0, tok, cols]) + bs_ref[g]
            ya_sc[tok, cols] = (u_ref[0, tok, cols].astype(F32) * mixed).astype(BF16)
    h = h_ref[0]
    ya = ya_sc[...]
    ob = ob_ref[0]
    oc = oc_ref[0]
    nc = 2 * LANES
    for c in range(0, D_MODEL, nc):
        cols = slice(c, c + nc)
        acc = jax.nn.sigmoid(_dot(h, wg_ref[:, c:c + nc])) * _dot(ya, wpa_ref[:, cols])
        acc += jax.nn.sigmoid(_dot(h, wg_ref[:, D_MODEL + c:D_MODEL + c + nc])) * _dot(ob, wpb_ref[:, cols])
        acc += jax.nn.sigmoid(_dot(h, wg_ref[:, 2 * D_MODEL + c:2 * D_MODEL + c + nc])) * _dot(oc, wpc_ref[:, cols])
        mg_sc[:, cols] = acc.astype(BF16)
    mix = _dot(mg_sc[...], wo_ref[...])
    out_ref[0] = x_ref[0] + _rms(mix, g_ref[...])


def _merge(x, h, u, v, ob, oc, lw):
    bsz, seq, _ = x.shape
    tm = min(TM_MERGE, seq)
    tok = pl.BlockSpec((1, tm, D_MODEL), lambda b, i: (b, i, 0))
    consts = [lw["w_gates"], lw["a_w_s"], lw["a_b_s"], lw["w_pa"], lw["w_pb"], lw["w_pc"], lw["w_o"],
              lw["post_mix_g"]]
    return pl.pallas_call(
        _merge_kernel,
        grid=(bsz, seq // tm),
        in_specs=[tok] * 6 + [_const_spec(a) for a in consts],
        out_specs=tok,
        out_shape=jax.ShapeDtypeStruct(x.shape, F32),
        scratch_shapes=[pltpu.VMEM((tm, A_WIDTH), BF16), pltpu.VMEM((tm, D_MODEL), BF16)],
        compiler_params=_params(2),
        name="merge",
    )(x, h, u, v, ob, oc, *consts)


def _ffn_kernel(x_ref, gpre_ref, wg_ref, wu_ref, wd_ref, gpost_ref, out_ref, a_sc):
    x = x_ref[0]
    h = _rms(x, gpre_ref[...]).astype(BF16)
    nc = 2 * LANES
    for c in range(0, D_FF, nc):
        gt = _dot(h, wg_ref[:, c:c + nc])
        a_sc[:, c:c + nc] = (gt * jax.nn.sigmoid(gt) * _dot(h, wu_ref[:, c:c + nc])).astype(BF16)
    f = _dot(a_sc[...], wd_ref[...])
    out_ref[0] = x + _rms(f, gpost_ref[...])


def _ffn(x, lw):
    bsz, seq, _ = x.shape
    tm = min(TM_FFN, seq)
    tok = pl.BlockSpec((1, tm, D_MODEL), lambda b, i: (b, i, 0))
    consts = [lw["pre_ffn_g"], lw["w_gate"], lw["w_up"], lw["w_down"], lw["post_ffn_g"]]
    return pl.pallas_call(
        _ffn_kernel,
        grid=(bsz, seq // tm),
        in_specs=[tok] + [_const_spec(a) for a in consts],
        out_specs=tok,
        out_shape=jax.ShapeDtypeStruct(x.shape, F32),
        scratch_shapes=[pltpu.VMEM((tm, D_FF), BF16)],
        compiler_params=_params(2),
        name="ffn",
    )(x, *consts)


def _rope_tables(seq):
    pos = jnp.arange(seq, dtype=F32)[:, None]

    def tables(dim):
        inv = 1.0 / (ROPE_THETA ** (jnp.arange(0, dim, 2, dtype=F32) / dim))
        ang = pos * inv[None, :]
        return jnp.cos(ang), jnp.sin(ang)

    cb, sb = tables(B_ROPE)
    zb = jnp.zeros((seq, LANES - B_ROPE), F32)
    cc, sc = tables(C_HD)
    return (jnp.concatenate([cb, cb, zb], axis=1), jnp.concatenate([-sb, sb, zb], axis=1),
            jnp.concatenate([cc, cc], axis=1), jnp.concatenate([-sc, sc], axis=1))


def _prep_layer(l, p):
    row = lambda a: a[l][None, :].astype(F32)
    w_in = p["w_in"][l]
    offs = [0]
    for s in IN_SIZES:
        offs.append(offs[-1] + s)
    seg = lambda k: w_in[:, offs[k]:offs[k + 1]]
    w_kr = seg(4)
    half = B_ROPE // 2
    zpad = jnp.zeros((D_MODEL, LANES - B_ROPE), F32)
    w_lat = jnp.concatenate([seg(2), seg(3), w_kr, zpad, w_kr[:, half:], w_kr[:, :half], zpad], axis=1)
    w_uq = p["b_w_uq"][l].reshape(B_Q_RANK, B_HEADS, B_NOPE + B_ROPE)
    q_pe = w_uq[:, :, B_NOPE:]
    zq = jnp.zeros((B_Q_RANK, B_HEADS, LANES - B_ROPE), F32)
    w_uq_main = jnp.concatenate([w_uq[:, :, :B_NOPE], q_pe, zq], axis=2).reshape(B_Q_RANK, B_HEADS * B_QK_PAD)
    w_uq_sw = jnp.concatenate([q_pe[:, :, half:], q_pe[:, :, :half], zq], axis=2).reshape(B_Q_RANK, B_HEADS * LANES)
    w_ukv = p["b_w_ukv"][l].reshape(B_KV_RANK, B_HEADS, B_NOPE + B_VDIM)
    return {
        "pre_mix_g": row(p["pre_mix_g"]),
        "w_uv": jnp.concatenate([seg(0), seg(1)], axis=1).astype(BF16),
        "w_lat": w_lat.astype(BF16),
        "w_c": jnp.concatenate([seg(5), seg(6), seg(7)], axis=1).astype(BF16),
        "w_gates": seg(8).astype(BF16),
        "a_ln_g": row(p["a_ln_g"]),
        "a_ln_b": row(p["a_ln_b"]),
        "b_q_norm_g": row(p["b_q_norm_g"]),
        "b_kv_norm_g": row(p["b_kv_norm_g"]),
        "w_uq": w_uq_main.astype(BF16),
        "w_uq_sw": w_uq_sw.astype(BF16),
        "w_uk": w_ukv[:, :, :B_NOPE].reshape(B_KV_RANK, B_HEADS * B_NOPE).astype(BF16),
        "w_uvv": w_ukv[:, :, B_NOPE:].reshape(B_KV_RANK, B_HEADS * B_VDIM).astype(BF16),
        "a_w_s": p["a_w_s"][l].astype(BF16),
        "a_b_s": jnp.broadcast_to(p["a_b_s"][l][:, :, None], (A_GROUPS, CHUNK, LANES)).astype(F32),
        "c_sink": jnp.broadcast_to((p["c_sink"][l] * LOG2E)[:, None], (C_HEADS, LANES)).astype(F32),
        "w_pa": p["w_pa"][l].astype(BF16),
        "w_pb": p["w_pb"][l].astype(BF16),
        "w_pc": p["w_pc"][l].astype(BF16),
        "w_o": p["w_o"][l].astype(BF16),
        "post_mix_g": row(p["post_mix_g"]),
        "pre_ffn_g": row(p["pre_ffn_g"]),
        "w_gate": p["w_gate"][l].astype(BF16),
        "w_up": p["w_up"][l].astype(BF16),
        "w_down": p["w_down"][l].astype(BF16),
        "post_ffn_g": row(p["post_ffn_g"]),
    }


def _trunk(x, layers):
    rope = _rope_tables(x.shape[1])
    for lw in layers:
        h, u, v, qm, km, vm, cq, ck, cv = _proj(x, lw, rope)
        ob = _mla(qm, km, vm)
        oc = _swa(cq, ck, cv, lw["c_sink"])
        x = _merge(x, h, u, v, ob, oc, lw)
        x = _ffn(x, lw)
    return x


def kernel(x_prompt, x_sample, pre_mix_g, w_in, a_ln_g, a_ln_b, a_w_s, a_b_s, b_q_norm_g, b_w_uq, b_kv_norm_g,
           b_w_ukv, c_sink, w_pa, w_pb, w_pc, w_o, post_mix_g, pre_ffn_g, w_gate, w_up, w_down, post_ffn_g):
    p = dict(pre_mix_g=pre_mix_g, w_in=w_in, a_ln_g=a_ln_g, a_ln_b=a_ln_b, a_w_s=a_w_s, a_b_s=a_b_s,
             b_q_norm_g=b_q_norm_g, b_w_uq=b_w_uq, b_kv_norm_g=b_kv_norm_g, b_w_ukv=b_w_ukv, c_sink=c_sink,
             w_pa=w_pa, w_pb=w_pb, w_pc=w_pc, w_o=w_o, post_mix_g=post_mix_g, pre_ffn_g=pre_ffn_g,
             w_gate=w_gate, w_up=w_up, w_down=w_down, post_ffn_g=post_ffn_g)
    layers = [_prep_layer(l, p) for l in range(w_in.shape[0])]
    return (_trunk(x_prompt, layers), _trunk(x_sample, layers))
```

```python
import functools
import math

import jax
import jax.numpy as jnp
from jax import lax
from jax.experimental import pallas as pl
from jax.experimental.pallas import tpu as pltpu

F32 = jnp.float32
BF16 = jnp.bfloat16

D_MODEL = 1024
EPS = 1e-6
ROPE_THETA = 10000.0
N_BRANCH = 3
CHUNK = 128
A_GROUPS = 8
A_WIDTH = 1024
B_HEADS = 8
B_Q_RANK = 384
B_KV_RANK = 256
B_NOPE = 128
B_ROPE = 64
B_VDIM = 128
C_HEADS = 8
C_KV_HEADS = 2
C_HD = 128
C_WIDTH = C_HEADS * C_HD
C_KV_WIDTH = C_KV_HEADS * C_HD
C_REP = C_HEADS // C_KV_HEADS
WINDOW = 128
D_FF = -(-8 * D_MODEL // (3 * 256)) * 256
IN_SIZES = (A_WIDTH, A_WIDTH, B_Q_RANK, B_KV_RANK, B_ROPE, C_WIDTH, C_KV_WIDTH, C_KV_WIDTH, N_BRANCH * D_MODEL)

LANES = 128
B_QK_PAD = 2 * LANES
LOG2E = math.log2(math.e)
B_QSCALE = (B_NOPE + B_ROPE) ** -0.5 * LOG2E
C_QSCALE = C_HD ** -0.5 * LOG2E

V7X_VMEM_BYTES = 64 * 1024 * 1024
VMEM_LIMIT = V7X_VMEM_BYTES - 8 * 1024 * 1024

TM_PROJ = 512
TM_MERGE = 512
TM_FFN = 512
TQ_MLA = 512
TK_MLA = 2048
MLA_QBLOCKS = 2
MLA_ROW_BLOCK = 64
SWA_BLOCKS = 16


def _dot(a, b):
    return jnp.dot(a, b, preferred_element_type=F32)


def _dot_nt(a, b):
    return lax.dot_general(a, b, (((1,), (1,)), ((), ())), preferred_element_type=F32)


def _rms(x, g):
    return x * lax.rsqrt(jnp.mean(x * x, axis=-1, keepdims=True) + EPS) * g


def _const_spec(arr):
    nd = arr.ndim
    return pl.BlockSpec(arr.shape, lambda *_: (0,) * nd, pipeline_mode=pl.Buffered(1))


def _params(n_axes, flags=None):
    return pltpu.CompilerParams(dimension_semantics=("arbitrary",) * n_axes, vmem_limit_bytes=VMEM_LIMIT,
                                flags=flags)


def _proj_kernel(x_ref, g_ref, wuv_ref, wlat_ref, wc_ref, lng_ref, lnb_ref, qg_ref, kvg_ref,
                 wuq_ref, wuqs_ref, wuk_ref, wuvv_ref, cb_ref, sb_ref, cc_ref, sc_ref,
                 h_ref, u_ref, v_ref, qm_ref, km_ref, vm_ref, cq_ref, ck_ref, cv_ref):
    h = _rms(x_ref[0], g_ref[...]).astype(BF16)
    h_ref[0] = h

    o1 = B_Q_RANK
    o2 = o1 + B_KV_RANK
    cqn = _rms(_dot(h, wlat_ref[:, :o1]), qg_ref[...]).astype(BF16)
    ckvn = _rms(_dot(h, wlat_ref[:, o1:o2]), kvg_ref[...]).astype(BF16)
    cb = cb_ref[...]
    sb = sb_ref[...]
    kr = _dot(h, wlat_ref[:, o2:])
    kpe = (kr[:, :LANES] * cb + kr[:, LANES:] * sb).astype(BF16)

    nc = 2 * LANES
    v = jnp.concatenate([jax.nn.gelu(_dot(h, wuv_ref[:, A_WIDTH + c:A_WIDTH + c + nc]))
                         for c in range(0, A_WIDTH, nc)], axis=1)
    vc = v - jnp.mean(v, axis=-1, keepdims=True)
    var = jnp.mean(vc * vc, axis=-1, keepdims=True)
    v_ref[0] = (vc * lax.rsqrt(var + EPS) * lng_ref[...] + lnb_ref[...]).astype(BF16)
    for c in range(0, A_WIDTH, nc):
        u_ref[0, :, c:c + nc] = jax.nn.gelu(_dot(h, wuv_ref[:, c:c + nc])).astype(BF16)

    cc = cc_ref[...]
    sc = sc_ref[...]
    for c in range(0, C_WIDTH + C_KV_WIDTH, nc):
        qk = _dot(h, wc_ref[:, c:c + nc])
        for j in range(nc // LANES):
            t = qk[:, j * LANES:(j + 1) * LANES]
            r = t * cc + pltpu.roll(t, C_HD // 2, 1) * sc
            col = c + j * LANES
            if col < C_WIDTH:
                cq_ref[0, :, col:col + LANES] = (r * C_QSCALE).astype(BF16)
            else:
                ck_ref[0, :, col - C_WIDTH:col - C_WIDTH + LANES] = r.astype(BF16)
    cv_ref[0] = _dot(h, wc_ref[:, C_WIDTH + C_KV_WIDTH:]).astype(BF16)

    for hp in range(0, B_HEADS, 2):
        qs2 = _dot(cqn, wuqs_ref[:, hp * LANES:(hp + 2) * LANES])
        for j in range(2):
            c0 = (hp + j) * B_QK_PAD
            qh = _dot(cqn, wuq_ref[:, c0:c0 + B_QK_PAD])
            qs = qs2[:, j * LANES:(j + 1) * LANES]
            qm_ref[0, :, c0:c0 + LANES] = (qh[:, :LANES] * B_QSCALE).astype(BF16)
            qm_ref[0, :, c0 + LANES:c0 + B_QK_PAD] = ((qh[:, LANES:] * cb + qs * sb) * B_QSCALE).astype(BF16)
            km_ref[0, :, c0 + LANES:c0 + B_QK_PAD] = kpe
        kn = _dot(ckvn, wuk_ref[:, hp * LANES:(hp + 2) * LANES])
        km_ref[0, :, hp * B_QK_PAD:hp * B_QK_PAD + LANES] = kn[:, :LANES].astype(BF16)
        km_ref[0, :, (hp + 1) * B_QK_PAD:(hp + 1) * B_QK_PAD + LANES] = kn[:, LANES:].astype(BF16)
        vm_ref[0, :, hp * LANES:(hp + 2) * LANES] = _dot(ckvn, wuvv_ref[:, hp * LANES:(hp + 2) * LANES]).astype(BF16)


def _proj(x, lw, rope):
    bsz, seq, _ = x.shape
    tm = min(TM_PROJ, seq)
    grid = (bsz, seq // tm)
    tok = lambda w: pl.BlockSpec((1, tm, w), lambda b, i: (b, i, 0))
    pos = pl.BlockSpec((tm, LANES), lambda b, i: (i, 0))
    consts = [lw["pre_mix_g"], lw["w_uv"], lw["w_lat"], lw["w_c"], lw["a_ln_g"], lw["a_ln_b"], lw["b_q_norm_g"],
              lw["b_kv_norm_g"], lw["w_uq"], lw["w_uq_sw"], lw["w_uk"], lw["w_uvv"]]
    widths = [D_MODEL, A_WIDTH, A_WIDTH, B_HEADS * B_QK_PAD, B_HEADS * B_QK_PAD, B_HEADS * B_VDIM,
              C_WIDTH, C_KV_WIDTH, C_KV_WIDTH]
    return pl.pallas_call(
        _proj_kernel,
        grid=grid,
        in_specs=[tok(D_MODEL)] + [_const_spec(a) for a in consts] + [pos] * 4,
        out_specs=[tok(w) for w in widths],
        out_shape=[jax.ShapeDtypeStruct((bsz, seq, w), BF16) for w in widths],
        compiler_params=_params(2),
        name="proj",
    )(x, *consts, *rope)


def _mla_kernel(q_ref, k_ref, v_ref, o_ref, *scratch, tq, tk):
    per_block = len(scratch) // MLA_QBLOCKS

    def group(i, carry):
        for qb in range(MLA_QBLOCKS):
            rows = pl.ds(pl.multiple_of((i * MLA_QBLOCKS + qb) * tq, tq), tq)
            _mla_block(q_ref.at[0, rows, :], k_ref, v_ref, o_ref.at[0, rows, :],
                       *scratch[qb * per_block:(qb + 1) * per_block], tk=tk)
        return carry

    lax.fori_loop(0, q_ref.shape[1] // (tq * MLA_QBLOCKS), group, 0)


def _mla_block(q_ref, k_ref, v_ref, o_ref, m_sc, acc_sc, s0_sc, s1_sc, p0_sc, p1_sc, a0_sc, a1_sc, *, tk):
    n = k_ref.shape[1] // tk
    s_sc = (s0_sc, s1_sc)
    p_sc = (p0_sc, p1_sc)
    a_sc = (a0_sc, a1_sc)
    m_sc[...] = jnp.full(m_sc.shape, -jnp.inf, F32)
    acc_sc[...] = jnp.zeros(acc_sc.shape, F32)
    ones = jnp.ones((tk, LANES), BF16)

    def scores(c, par):
        s_sc[par][...] = _dot_nt(q_ref[...], k_ref[0, c * tk:(c + 1) * tk, :])

    def softmax(par):
        for r0 in range(0, q_ref.shape[0], MLA_ROW_BLOCK):
            rows = slice(r0, r0 + MLA_ROW_BLOCK)
            chunks = [s_sc[par][rows, c * LANES:(c + 1) * LANES] for c in range(tk // LANES)]
            m_prev = m_sc[rows, :]
            m_new = jnp.maximum(m_prev, jnp.max(functools.reduce(jnp.maximum, chunks), axis=-1, keepdims=True))
            m_sc[rows, :] = m_new
            a_sc[par][rows, :] = jnp.exp2(m_prev - m_new)
            for c, sc in enumerate(chunks):
                p_sc[par][rows, c * LANES:(c + 1) * LANES] = jnp.exp2((sc - m_new).astype(BF16))

    def values(c, par):
        v1 = jnp.concatenate([v_ref[0, c * tk:(c + 1) * tk, :], ones], axis=1)
        alpha = a_sc[par][...]
        acc_sc[...] = jnp.concatenate([alpha, alpha], axis=1) * acc_sc[...] + _dot(p_sc[par][...], v1)

    scores(0, 0)
    for t in range(n - 1):
        scores(t + 1, (t + 1) % 2)
        softmax(t % 2)
        values(t, t % 2)
    softmax((n - 1) % 2)
    values(n - 1, (n - 1) % 2)
    o_ref[...] = (acc_sc[:, :B_VDIM] / acc_sc[:, B_VDIM:]).astype(BF16)


def _mla(qm, km, vm):
    bsz, seq, _ = qm.shape
    tq = min(TQ_MLA, seq // MLA_QBLOCKS)
    tk = min(TK_MLA, seq // 2)
    assert seq % (tq * MLA_QBLOCKS) == 0
    row_stat = pltpu.VMEM((tq, LANES), F32)
    block_scratch = [row_stat, pltpu.VMEM((tq, B_VDIM + LANES), F32),
                     pltpu.VMEM((tq, tk), F32), pltpu.VMEM((tq, tk), F32),
                     pltpu.VMEM((tq, tk), BF16), pltpu.VMEM((tq, tk), BF16), row_stat, row_stat]
    return pl.pallas_call(
        functools.partial(_mla_kernel, tq=tq, tk=tk),
        grid=(bsz, B_HEADS),
        in_specs=[
            pl.BlockSpec((1, seq, B_QK_PAD), lambda b, h: (b, 0, h)),
            pl.BlockSpec((1, seq, B_QK_PAD), lambda b, h: (b, 0, h)),
            pl.BlockSpec((1, seq, B_VDIM), lambda b, h: (b, 0, h)),
        ],
        out_specs=pl.BlockSpec((1, seq, B_VDIM), lambda b, h: (b, 0, h)),
        out_shape=jax.ShapeDtypeStruct((bsz, seq, B_HEADS * B_VDIM), BF16),
        scratch_shapes=block_scratch * MLA_QBLOCKS,
        compiler_params=_params(2),
        name="mla",
    )(qm, km, vm)


def _swa_kernel(q_ref, kc_ref, kp_ref, kn_ref, vc_ref, vp_ref, vn_ref, sink_ref, o_ref, *, nblk):
    i = pl.program_id(1)
    last = pl.num_programs(1) - 1
    rows = C_REP * CHUNK
    qi = lax.broadcasted_iota(jnp.int32, (rows, CHUNK), 0) % CHUNK
    kj = lax.broadcasted_iota(jnp.int32, (rows, CHUNK), 1)
    band_prev = kj >= qi
    band_next = kj <= qi
    neg = jnp.float32(-jnp.inf)
    ones = jnp.ones((3 * CHUNK, LANES), BF16)

    def blk(ref_c, ref_e, n, g, edge_lo):
        cols = slice(g * C_HD, (g + 1) * C_HD)
        if edge_lo:
            return ref_e[0, :, cols] if n == 0 else ref_c[0, (n - 1) * CHUNK:n * CHUNK, cols]
        return ref_e[0, :, cols] if n == nblk - 1 else ref_c[0, (n + 1) * CHUNK:(n + 2) * CHUNK, cols]

    for n in range(nblk):
        tok = slice(n * CHUNK, (n + 1) * CHUNK)
        for g in range(C_KV_HEADS):
            cols = slice(g * C_HD, (g + 1) * C_HD)
            q4 = jnp.concatenate([q_ref[0, tok, (g * C_REP + r) * C_HD:(g * C_REP + r + 1) * C_HD]
                                  for r in range(C_REP)], axis=0)
            sink = jnp.concatenate([jnp.broadcast_to(sink_ref[g * C_REP + r:g * C_REP + r + 1, :], (CHUNK, LANES))
                                    for r in range(C_REP)], axis=0)
            k3 = jnp.concatenate([blk(kc_ref, kp_ref, n, g, True), kc_ref[0, tok, cols],
                                  blk(kc_ref, kn_ref, n, g, False)], axis=0)
            v3 = jnp.concatenate([blk(vc_ref, vp_ref, n, g, True), vc_ref[0, tok, cols],
                                  blk(vc_ref, vn_ref, n, g, False)], axis=0)
            s = _dot_nt(q4, k3)
            ok_prev = band_prev if n > 0 else jnp.logical_and(band_prev, i > 0)
            ok_next = band_next if n < nblk - 1 else jnp.logical_and(band_next, i < last)
            s0 = jnp.where(ok_prev, s[:, :CHUNK], neg)
            s1 = s[:, CHUNK:2 * CHUNK]
            s2 = jnp.where(ok_next, s[:, 2 * CHUNK:], neg)
            m = jnp.max(jnp.maximum(jnp.maximum(s0, s1), s2), axis=-1, keepdims=True)
            m = jnp.maximum(m, sink)
            p = jnp.concatenate([jnp.exp2((t - m).astype(BF16)) for t in (s0, s1, s2)], axis=1)
            pv = _dot(p, jnp.concatenate([v3, ones], axis=1))
            o = pv[:, :C_HD] / (pv[:, C_HD:] + jnp.exp2(sink - m))
            for r in range(C_REP):
                hc = (g * C_REP + r) * C_HD
                o_ref[0, tok, hc:hc + C_HD] = o[r * CHUNK:(r + 1) * CHUNK].astype(BF16)


def _swa(cq, ck, cv, sink):
    bsz, seq, _ = cq.shape
    nblk = min(SWA_BLOCKS, seq // CHUNK)
    ts = nblk * CHUNK
    nb = seq // CHUNK
    cur = lambda w: pl.BlockSpec((1, ts, w), lambda b, i: (b, i, 0))
    prev = pl.BlockSpec((1, CHUNK, C_KV_WIDTH), lambda b, i: (b, jnp.maximum(i * nblk - 1, 0), 0))
    nxt = pl.BlockSpec((1, CHUNK, C_KV_WIDTH), lambda b, i: (b, jnp.minimum((i + 1) * nblk, nb - 1), 0))
    return pl.pallas_call(
        functools.partial(_swa_kernel, nblk=nblk),
        grid=(bsz, seq // ts),
        in_specs=[cur(C_WIDTH), cur(C_KV_WIDTH), prev, nxt, cur(C_KV_WIDTH), prev, nxt, _const_spec(sink)],
        out_specs=cur(C_WIDTH),
        out_shape=jax.ShapeDtypeStruct((bsz, seq, C_WIDTH), BF16),
        compiler_params=_params(2),
        name="swa",
    )(cq, ck, ck, ck, cv, cv, cv, sink)


def _merge_kernel(x_ref, h_ref, u_ref, v_ref, ob_ref, oc_ref, wg_ref, ws_ref, bs_ref, wpa_ref, wpb_ref, wpc_ref,
                  wo_ref, g_ref, out_ref, ya_sc, mg_sc):
    tm = x_ref.shape[1]
    for c in range(tm // CHUNK):
        tok = slice(c * CHUNK, (c + 1) * CHUNK)
        for g in range(A_GROUPS):
            cols = slice(g * LANES, (g + 1) * LANES)
            mixed = _dot(ws_ref[g], v_ref[0, tok, cols]) + bs_ref[g]
            ya_sc[tok, cols] = (u_ref[0, tok, cols].astype(F32) * mixed).astype(BF16)
    h = h_ref[0]
    ya = ya_sc[...]
    ob = ob_ref[0]
    oc = oc_ref[0]
    nc = 2 * LANES
    for c in range(0, D_MODEL, nc):
        cols = slice(c, c + nc)
        acc = jax.nn.sigmoid(_dot(h, wg_ref[:, c:c + nc])) * _dot(ya, wpa_ref[:, cols])
        acc += jax.nn.sigmoid(_dot(h, wg_ref[:, D_MODEL + c:D_MODEL + c + nc])) * _dot(ob, wpb_ref[:, cols])
        acc += jax.nn.sigmoid(_dot(h, wg_ref[:, 2 * D_MODEL + c:2 * D_MODEL + c + nc])) * _dot(oc, wpc_ref[:, cols])
        mg_sc[:, cols] = acc.astype(BF16)
    mix = _dot(mg_sc[...], wo_ref[...])
    out_ref[0] = x_ref[0] + _rms(mix, g_ref[...])


def _merge(x, h, u, v, ob, oc, lw):
    bsz, seq, _ = x.shape
    tm = min(TM_MERGE, seq)
    tok = pl.BlockSpec((1, tm, D_MODEL), lambda b, i: (b, i, 0))
    consts = [lw["w_gates"], lw["a_w_s"], lw["a_b_s"], lw["w_pa"], lw["w_pb"], lw["w_pc"], lw["w_o"],
              lw["post_mix_g"]]
    return pl.pallas_call(
        _merge_kernel,
        grid=(bsz, seq // tm),
        in_specs=[tok] * 6 + [_const_spec(a) for a in consts],
        out_specs=tok,
        out_shape=jax.ShapeDtypeStruct(x.shape, F32),
        scratch_shapes=[pltpu.VMEM((tm, A_WIDTH), BF16), pltpu.VMEM((tm, D_MODEL), BF16)],
        compiler_params=_params(2),
        name="merge",
    )(x, h, u, v, ob, oc, *consts)


def _ffn_kernel(x_ref, gpre_ref, wg_ref, wu_ref, wd_ref, gpost_ref, out_ref, a_sc):
    x = x_ref[0]
    h = _rms(x, gpre_ref[...]).astype(BF16)
    nc = 2 * LANES
    for c in range(0, D_FF, nc):
        gt = _dot(h, wg_ref[:, c:c + nc])
        a_sc[:, c:c + nc] = (gt * jax.nn.sigmoid(gt) * _dot(h, wu_ref[:, c:c + nc])).astype(BF16)
    f = _dot(a_sc[...], wd_ref[...])
    out_ref[0] = x + _rms(f, gpost_ref[...])


def _ffn(x, lw):
    bsz, seq, _ = x.shape
    tm = min(TM_FFN, seq)
    tok = pl.BlockSpec((1, tm, D_MODEL), lambda b, i: (b, i, 0))
    consts = [lw["pre_ffn_g"], lw["w_gate"], lw["w_up"], lw["w_down"], lw["post_ffn_g"]]
    return pl.pallas_call(
        _ffn_kernel,
        grid=(bsz, seq // tm),
        in_specs=[tok] + [_const_spec(a) for a in consts],
        out_specs=tok,
        out_shape=jax.ShapeDtypeStruct(x.shape, F32),
        scratch_shapes=[pltpu.VMEM((tm, D_FF), BF16)],
        compiler_params=_params(2),
        name="ffn",
    )(x, *consts)


def _rope_tables(seq):
    pos = jnp.arange(seq, dtype=F32)[:, None]

    def tables(dim):
        inv = 1.0 / (ROPE_THETA ** (jnp.arange(0, dim, 2, dtype=F32) / dim))
        ang = pos * inv[None, :]
        return jnp.cos(ang), jnp.sin(ang)

    cb, sb = tables(B_ROPE)
    zb = jnp.zeros((seq, LANES - B_ROPE), F32)
    cc, sc = tables(C_HD)
    return (jnp.concatenate([cb, cb, zb], axis=1), jnp.concatenate([-sb, sb, zb], axis=1),
            jnp.concatenate([cc, cc], axis=1), jnp.concatenate([-sc, sc], axis=1))


def _prep_layer(l, p):
    row = lambda a: a[l][None, :].astype(F32)
    w_in = p["w_in"][l]
    offs = [0]
    for s in IN_SIZES:
        offs.append(offs[-1] + s)
    seg = lambda k: w_in[:, offs[k]:offs[k + 1]]
    w_kr = seg(4)
    half = B_ROPE // 2
    zpad = jnp.zeros((D_MODEL, LANES - B_ROPE), F32)
    w_lat = jnp.concatenate([seg(2), seg(3), w_kr, zpad, w_kr[:, half:], w_kr[:, :half], zpad], axis=1)
    w_uq = p["b_w_uq"][l].reshape(B_Q_RANK, B_HEADS, B_NOPE + B_ROPE)
    q_pe = w_uq[:, :, B_NOPE:]
    zq = jnp.zeros((B_Q_RANK, B_HEADS, LANES - B_ROPE), F32)
    w_uq_main = jnp.concatenate([w_uq[:, :, :B_NOPE], q_pe, zq], axis=2).reshape(B_Q_RANK, B_HEADS * B_QK_PAD)
    w_uq_sw = jnp.concatenate([q_pe[:, :, half:], q_pe[:, :, :half], zq], axis=2).reshape(B_Q_RANK, B_HEADS * LANES)
    w_ukv = p["b_w_ukv"][l].reshape(B_KV_RANK, B_HEADS, B_NOPE + B_VDIM)
    return {
        "pre_mix_g": row(p["pre_mix_g"]),
        "w_uv": jnp.concatenate([seg(0), seg(1)], axis=1).astype(BF16),
        "w_lat": w_lat.astype(BF16),
        "w_c": jnp.concatenate([seg(5), seg(6), seg(7)], axis=1).astype(BF16),
        "w_gates": seg(8).astype(BF16),
        "a_ln_g": row(p["a_ln_g"]),
        "a_ln_b": row(p["a_ln_b"]),
        "b_q_norm_g": row(p["b_q_norm_g"]),
        "b_kv_norm_g": row(p["b_kv_norm_g"]),
        "w_uq": w_uq_main.astype(BF16),
        "w_uq_sw": w_uq_sw.astype(BF16),
        "w_uk": w_ukv[:, :, :B_NOPE].reshape(B_KV_RANK, B_HEADS * B_NOPE).astype(BF16),
        "w_uvv": w_ukv[:, :, B_NOPE:].reshape(B_KV_RANK, B_HEADS * B_VDIM).astype(BF16),
        "a_w_s": p["a_w_s"][l].astype(BF16),
        "a_b_s": jnp.broadcast_to(p["a_b_s"][l][:, :, None], (A_GROUPS, CHUNK, LANES)).astype(F32),
        "c_sink": jnp.broadcast_to((p["c_sink"][l] * LOG2E)[:, None], (C_HEADS, LANES)).astype(F32),
        "w_pa": p["w_pa"][l].astype(BF16),
        "w_pb": p["w_pb"][l].astype(BF16),
        "w_pc": p["w_pc"][l].astype(BF16),
        "w_o": p["w_o"][l].astype(BF16),
        "post_mix_g": row(p["post_mix_g"]),
        "pre_ffn_g": row(p["pre_ffn_g"]),
        "w_gate": p["w_gate"][l].astype(BF16),
        "w_up": p["w_up"][l].astype(BF16),
        "w_down": p["w_down"][l].astype(BF16),
        "post_ffn_g": row(p["post_ffn_g"]),
    }


def _trunk(x, layers):
    rope = _rope_tables(x.shape[1])
    for lw in layers:
        h, u, v, qm, km, vm, cq, ck, cv = _proj(x, lw, rope)
        ob = _mla(qm, km, vm)
        oc = _swa(cq, ck, cv, lw["c_sink"])
        x = _merge(x, h, u, v, ob, oc, lw)
        x = _ffn(x, lw)
    return x


def kernel(x_prompt, x_sample, pre_mix_g, w_in, a_ln_g, a_ln_b, a_w_s, a_b_s, b_q_norm_g, b_w_uq, b_kv_norm_g,
           b_w_ukv, c_sink, w_pa, w_pb, w_pc, w_o, post_mix_g, pre_ffn_g, w_gate, w_up, w_down, post_ffn_g):
    p = dict(pre_mix_g=pre_mix_g, w_in=w_in, a_ln_g=a_ln_g, a_ln_b=a_ln_b, a_w_s=a_w_s, a_b_s=a_b_s,
             b_q_norm_g=b_q_norm_g, b_w_uq=b_w_uq, b_kv_norm_g=b_kv_norm_g, b_w_ukv=b_w_ukv, c_sink=c_sink,
             w_pa=w_pa, w_pb=w_pb, w_pc=w_pc, w_o=w_o, post_mix_g=post_mix_g, pre_ffn_g=pre_ffn_g,
             w_gate=w_gate, w_up=w_up, w_down=w_down, post_ffn_g=post_ffn_g)
    layers = [_prep_layer(l, p) for l in range(w_in.shape[0])]
    return (_trunk(x_prompt, layers), _trunk(x_sample, layers))
```

```python
import functools
import math

import jax
import jax.numpy as jnp
from jax import lax
from jax.experimental import pallas as pl
from jax.experimental.pallas import tpu as pltpu

F32 = jnp.float32
BF16 = jnp.bfloat16

D_MODEL = 1024
EPS = 1e-6
ROPE_THETA = 10000.0
N_BRANCH = 3
CHUNK = 128
A_GROUPS = 8
A_WIDTH = 1024
B_HEADS = 8
B_Q_RANK = 384
B_KV_RANK = 256
B_NOPE = 128
B_ROPE = 64
B_VDIM = 128
C_HEADS = 8
C_KV_HEADS = 2
C_HD = 128
C_WIDTH = C_HEADS * C_HD
C_KV_WIDTH = C_KV_HEADS * C_HD
C_REP = C_HEADS // C_KV_HEADS
WINDOW = 128
D_FF = -(-8 * D_MODEL // (3 * 256)) * 256
IN_SIZES = (A_WIDTH, A_WIDTH, B_Q_RANK, B_KV_RANK, B_ROPE, C_WIDTH, C_KV_WIDTH, C_KV_WIDTH, N_BRANCH * D_MODEL)

LANES = 128
B_QK_PAD = 2 * LANES
LOG2E = math.log2(math.e)
B_QSCALE = (B_NOPE + B_ROPE) ** -0.5 * LOG2E
C_QSCALE = C_HD ** -0.5 * LOG2E

V7X_VMEM_BYTES = 64 * 1024 * 1024
VMEM_LIMIT = V7X_VMEM_BYTES - 8 * 1024 * 1024

TM_PROJ = 512
TM_MERGE = 512
TM_FFN = 1024
FFN_SUBBLOCKS = 2
TQ_MLA = 512
TK_MLA = 2048
MLA_QBLOCKS = 2
SWA_BLOCKS = 16


def _dot(a, b):
    return jnp.dot(a, b, preferred_element_type=F32)


def _dot_nt(a, b):
    return lax.dot_general(a, b, (((1,), (1,)), ((), ())), preferred_element_type=F32)


def _rms(x, g):
    return x * lax.rsqrt(jnp.mean(x * x, axis=-1, keepdims=True) + EPS) * g


def _const_spec(arr):
    nd = arr.ndim
    return pl.BlockSpec(arr.shape, lambda *_: (0,) * nd, pipeline_mode=pl.Buffered(1))


def _params(n_axes, flags=None):
    return pltpu.CompilerParams(dimension_semantics=("arbitrary",) * n_axes, vmem_limit_bytes=VMEM_LIMIT,
                                flags=flags)


def _proj_kernel(x_ref, g_ref, wuv_ref, wlat_ref, wc_ref, lng_ref, lnb_ref, qg_ref, kvg_ref,
                 wuq_ref, wuqs_ref, wuk_ref, wuvv_ref, cb_ref, sb_ref, cc_ref, sc_ref,
                 h_ref, u_ref, v_ref, qm_ref, km_ref, vm_ref, cq_ref, ck_ref, cv_ref):
    h = _rms(x_ref[0], g_ref[...]).astype(BF16)
    h_ref[0] = h

    o1 = B_Q_RANK
    o2 = o1 + B_KV_RANK
    cqn = _rms(_dot(h, wlat_ref[:, :o1]), qg_ref[...]).astype(BF16)
    ckvn = _rms(_dot(h, wlat_ref[:, o1:o2]), kvg_ref[...]).astype(BF16)
    cb = cb_ref[...]
    sb = sb_ref[...]
    kr = _dot(h, wlat_ref[:, o2:])
    kpe = (kr[:, :LANES] * cb + kr[:, LANES:] * sb).astype(BF16)

    nc = 2 * LANES
    v = jnp.concatenate([jax.nn.gelu(_dot(h, wuv_ref[:, A_WIDTH + c:A_WIDTH + c + nc]))
                         for c in range(0, A_WIDTH, nc)], axis=1)
    vc = v - jnp.mean(v, axis=-1, keepdims=True)
    var = jnp.mean(vc * vc, axis=-1, keepdims=True)
    v_ref[0] = (vc * lax.rsqrt(var + EPS) * lng_ref[...] + lnb_ref[...]).astype(BF16)
    for c in range(0, A_WIDTH, nc):
        u_ref[0, :, c:c + nc] = jax.nn.gelu(_dot(h, wuv_ref[:, c:c + nc])).astype(BF16)

    cc = cc_ref[...]
    sc = sc_ref[...]
    for c in range(0, C_WIDTH + C_KV_WIDTH, nc):
        qk = _dot(h, wc_ref[:, c:c + nc])
        for j in range(nc // LANES):
            t = qk[:, j * LANES:(j + 1) * LANES]
            r = t * cc + pltpu.roll(t, C_HD // 2, 1) * sc
            col = c + j * LANES
            if col < C_WIDTH:
                cq_ref[0, :, col:col + LANES] = (r * C_QSCALE).astype(BF16)
            else:
                ck_ref[0, :, col - C_WIDTH:col - C_WIDTH + LANES] = r.astype(BF16)
    cv_ref[0] = _dot(h, wc_ref[:, C_WIDTH + C_KV_WIDTH:]).astype(BF16)

    for hp in range(0, B_HEADS, 2):
        qs2 = _dot(cqn, wuqs_ref[:, hp * LANES:(hp + 2) * LANES])
        for j in range(2):
            c0 = (hp + j) * B_QK_PAD
            qh = _dot(cqn, wuq_ref[:, c0:c0 + B_QK_PAD])
            qs = qs2[:, j * LANES:(j + 1) * LANES]
            qm_ref[0, :, c0:c0 + LANES] = (qh[:, :LANES] * B_QSCALE).astype(BF16)
            qm_ref[0, :, c0 + LANES:c0 + B_QK_PAD] = ((qh[:, LANES:] * cb + qs * sb) * B_QSCALE).astype(BF16)
            km_ref[0, :, c0 + LANES:c0 + B_QK_PAD] = kpe
        kn = _dot(ckvn, wuk_ref[:, hp * LANES:(hp + 2) * LANES])
        km_ref[0, :, hp * B_QK_PAD:hp * B_QK_PAD + LANES] = kn[:, :LANES].astype(BF16)
        km_ref[0, :, (hp + 1) * B_QK_PAD:(hp + 1) * B_QK_PAD + LANES] = kn[:, LANES:].astype(BF16)
        vm_ref[0, :, hp * LANES:(hp + 2) * LANES] = _dot(ckvn, wuvv_ref[:, hp * LANES:(hp + 2) * LANES]).astype(BF16)


def _proj(x, lw, rope):
    bsz, seq, _ = x.shape
    tm = min(TM_PROJ, seq)
    grid = (bsz, seq // tm)
    tok = lambda w: pl.BlockSpec((1, tm, w), lambda b, i: (b, i, 0))
    pos = pl.BlockSpec((tm, LANES), lambda b, i: (i, 0))
    consts = [lw["pre_mix_g"], lw["w_uv"], lw["w_lat"], lw["w_c"], lw["a_ln_g"], lw["a_ln_b"], lw["b_q_norm_g"],
              lw["b_kv_norm_g"], lw["w_uq"], lw["w_uq_sw"], lw["w_uk"], lw["w_uvv"]]
    widths = [D_MODEL, A_WIDTH, A_WIDTH, B_HEADS * B_QK_PAD, B_HEADS * B_QK_PAD, B_HEADS * B_VDIM,
              C_WIDTH, C_KV_WIDTH, C_KV_WIDTH]
    return pl.pallas_call(
        _proj_kernel,
        grid=grid,
        in_specs=[tok(D_MODEL)] + [_const_spec(a) for a in consts] + [pos] * 4,
        out_specs=[tok(w) for w in widths],
        out_shape=[jax.ShapeDtypeStruct((bsz, seq, w), BF16) for w in widths],
        compiler_params=_params(2),
        name="proj",
    )(x, *consts, *rope)


def _mla_kernel(q_ref, k_ref, v_ref, o_ref, *, tq, tk):
    def group(i, carry):
        for qb in range(MLA_QBLOCKS):
            rows = pl.ds(pl.multiple_of((i * MLA_QBLOCKS + qb) * tq, tq), tq)
            o_ref[0, rows, :] = _mla_block(q_ref[0, rows, :], k_ref, v_ref, tk)
        return carry

    lax.fori_loop(0, q_ref.shape[1] // (tq * MLA_QBLOCKS), group, 0)


def _mla_block(q, k_ref, v_ref, tk):
    n = k_ref.shape[1] // tk
    tq = q.shape[0]
    ones = jnp.ones((tk, LANES), BF16)

    def scores(c):
        return _dot_nt(q, k_ref[0, c * tk:(c + 1) * tk, :])

    m = jnp.full((tq, LANES), -jnp.inf, F32)
    acc = jnp.zeros((tq, B_VDIM + LANES), F32)
    s = scores(0)
    for t in range(n):
        s_next = scores(t + 1) if t + 1 < n else None
        chunks = [s[:, c * LANES:(c + 1) * LANES] for c in range(tk // LANES)]
        m_new = jnp.maximum(m, jnp.max(functools.reduce(jnp.maximum, chunks), axis=-1, keepdims=True))
        alpha = jnp.exp2(m - m_new)
        p = jnp.concatenate([jnp.exp2((c - m_new).astype(BF16)) for c in chunks], axis=1)
        v1 = jnp.concatenate([v_ref[0, t * tk:(t + 1) * tk, :], ones], axis=1)
        acc = jnp.concatenate([alpha, alpha], axis=1) * acc + _dot(p, v1)
        m = m_new
        s = s_next
    return (acc[:, :B_VDIM] / acc[:, B_VDIM:]).astype(BF16)


def _mla(qm, km, vm):
    bsz, seq, _ = qm.shape
    tq = min(TQ_MLA, seq // MLA_QBLOCKS)
    tk = min(TK_MLA, seq // 2)
    assert seq % (tq * MLA_QBLOCKS) == 0
    return pl.pallas_call(
        functools.partial(_mla_kernel, tq=tq, tk=tk),
        grid=(bsz, B_HEADS),
        in_specs=[
            pl.BlockSpec((1, seq, B_QK_PAD), lambda b, h: (b, 0, h)),
            pl.BlockSpec((1, seq, B_QK_PAD), lambda b, h: (b, 0, h)),
            pl.BlockSpec((1, seq, B_VDIM), lambda b, h: (b, 0, h)),
        ],
        out_specs=pl.BlockSpec((1, seq, B_VDIM), lambda b, h: (b, 0, h)),
        out_shape=jax.ShapeDtypeStruct((bsz, seq, B_HEADS * B_VDIM), BF16),
        compiler_params=_params(2),
        name="mla",
    )(qm, km, vm)


def _swa_kernel(q_ref, kc_ref, kp_ref, kn_ref, vc_ref, vp_ref, vn_ref, sink_ref, o_ref, *, nblk):
    i = pl.program_id(1)
    last = pl.num_programs(1) - 1
    rows = C_REP * CHUNK
    qi = lax.broadcasted_iota(jnp.int32, (rows, CHUNK), 0) % CHUNK
    kj = lax.broadcasted_iota(jnp.int32, (rows, CHUNK), 1)
    band_prev = kj >= qi
    band_next = kj <= qi
    neg = jnp.float32(-jnp.inf)
    ones = jnp.ones((3 * CHUNK, LANES), BF16)

    def blk(ref_c, ref_e, n, g, edge_lo):
        cols = slice(g * C_HD, (g + 1) * C_HD)
        if edge_lo:
            return ref_e[0, :, cols] if n == 0 else ref_c[0, (n - 1) * CHUNK:n * CHUNK, cols]
        return ref_e[0, :, cols] if n == nblk - 1 else ref_c[0, (n + 1) * CHUNK:(n + 2) * CHUNK, cols]

    for n in range(nblk):
        tok = slice(n * CHUNK, (n + 1) * CHUNK)
        for g in range(C_KV_HEADS):
            cols = slice(g * C_HD, (g + 1) * C_HD)
            q4 = jnp.concatenate([q_ref[0, tok, (g * C_REP + r) * C_HD:(g * C_REP + r + 1) * C_HD]
                                  for r in range(C_REP)], axis=0)
            sink = jnp.concatenate([jnp.broadcast_to(sink_ref[g * C_REP + r:g * C_REP + r + 1, :], (CHUNK, LANES))
                                    for r in range(C_REP)], axis=0)
            k3 = jnp.concatenate([blk(kc_ref, kp_ref, n, g, True), kc_ref[0, tok, cols],
                                  blk(kc_ref, kn_ref, n, g, False)], axis=0)
            v3 = jnp.concatenate([blk(vc_ref, vp_ref, n, g, True), vc_ref[0, tok, cols],
                                  blk(vc_ref, vn_ref, n, g, False)], axis=0)
            s = _dot_nt(q4, k3)
            ok_prev = band_prev if n > 0 else jnp.logical_and(band_prev, i > 0)
            ok_next = band_next if n < nblk - 1 else jnp.logical_and(band_next, i < last)
            s0 = jnp.where(ok_prev, s[:, :CHUNK], neg)
            s1 = s[:, CHUNK:2 * CHUNK]
            s2 = jnp.where(ok_next, s[:, 2 * CHUNK:], neg)
            m = jnp.max(jnp.maximum(jnp.maximum(s0, s1), s2), axis=-1, keepdims=True)
            m = jnp.maximum(m, sink)
            p = jnp.concatenate([jnp.exp2((t - m).astype(BF16)) for t in (s0, s1, s2)], axis=1)
            pv = _dot(p, jnp.concatenate([v3, ones], axis=1))
            o = pv[:, :C_HD] / (pv[:, C_HD:] + jnp.exp2(sink - m))
            for r in range(C_REP):
                hc = (g * C_REP + r) * C_HD
                o_ref[0, tok, hc:hc + C_HD] = o[r * CHUNK:(r + 1) * CHUNK].astype(BF16)


def _swa(cq, ck, cv, sink):
    bsz, seq, _ = cq.shape
    nblk = min(SWA_BLOCKS, seq // CHUNK)
    ts = nblk * CHUNK
    nb = seq // CHUNK
    cur = lambda w: pl.BlockSpec((1, ts, w), lambda b, i: (b, i, 0))
    prev = pl.BlockSpec((1, CHUNK, C_KV_WIDTH), lambda b, i: (b, jnp.maximum(i * nblk - 1, 0), 0))
    nxt = pl.BlockSpec((1, CHUNK, C_KV_WIDTH), lambda b, i: (b, jnp.minimum((i + 1) * nblk, nb - 1), 0))
    return pl.pallas_call(
        functools.partial(_swa_kernel, nblk=nblk),
        grid=(bsz, seq // ts),
        in_specs=[cur(C_WIDTH), cur(C_KV_WIDTH), prev, nxt, cur(C_KV_WIDTH), prev, nxt, _const_spec(sink)],
        out_specs=cur(C_WIDTH),
        out_shape=jax.ShapeDtypeStruct((bsz, seq, C_WIDTH), BF16),
        compiler_params=_params(2),
        name="swa",
    )(cq, ck, ck, ck, cv, cv, cv, sink)


def _merge_kernel(x_ref, h_ref, u_ref, v_ref, ob_ref, oc_ref, wg_ref, ws_ref, bs_ref, wpa_ref, wpb_ref, wpc_ref,
                  wo_ref, g_ref, out_ref, ya_sc, mg_sc):
    tm = x_ref.shape[1]
    for c in range(tm // CHUNK):
        tok = slice(c * CHUNK, (c + 1) * CHUNK)
        for g in range(A_GROUPS):
            cols = slice(g * LANES, (g + 1) * LANES)
            mixed = _dot(ws_ref[g], v_ref[0, tok, cols]) + bs_ref[g]
            ya_sc[tok, cols] = (u_ref[0, tok, cols].astype(F32) * mixed).astype(BF16)
    h = h_ref[0]
    ya = ya_sc[...]
    ob = ob_ref[0]
    oc = oc_ref[0]
    nc = 2 * LANES
    for c in range(0, D_MODEL, nc):
        cols = slice(c, c + nc)
        acc = jax.nn.sigmoid(_dot(h, wg_ref[:, c:c + nc])) * _dot(ya, wpa_ref[:, cols])
        acc += jax.nn.sigmoid(_dot(h, wg_ref[:, D_MODEL + c:D_MODEL + c + nc])) * _dot(ob, wpb_ref[:, cols])
        acc += jax.nn.sigmoid(_dot(h, wg_ref[:, 2 * D_MODEL + c:2 * D_MODEL + c + nc])) * _dot(oc, wpc_ref[:, cols])
        mg_sc[:, cols] = acc.astype(BF16)
    mix = _dot(mg_sc[...], wo_ref[...])
    out_ref[0] = x_ref[0] + _rms(mix, g_ref[...])


def _merge(x, h, u, v, ob, oc, lw):
    bsz, seq, _ = x.shape
    tm = min(TM_MERGE, seq)
    tok = pl.BlockSpec((1, tm, D_MODEL), lambda b, i: (b, i, 0))
    consts = [lw["w_gates"], lw["a_w_s"], lw["a_b_s"], lw["w_pa"], lw["w_pb"], lw["w_pc"], lw["w_o"],
              lw["post_mix_g"]]
    return pl.pallas_call(
        _merge_kernel,
        grid=(bsz, seq // tm),
        in_specs=[tok] * 6 + [_const_spec(a) for a in consts],
        out_specs=tok,
        out_shape=jax.ShapeDtypeStruct(x.shape, F32),
        scratch_shapes=[pltpu.VMEM((tm, A_WIDTH), BF16), pltpu.VMEM((tm, D_MODEL), BF16)],
        compiler_params=_params(2),
        name="merge",
    )(x, h, u, v, ob, oc, *consts)


def _ffn_kernel(x_ref, gpre_ref, wg_ref, wu_ref, wd_ref, gpost_ref, out_ref, *a_scs):
    sub = x_ref.shape[1] // len(a_scs)
    nc = 2 * LANES
    for sb, a_sc in enumerate(a_scs):
        rows = slice(sb * sub, (sb + 1) * sub)
        x = x_ref[0, rows, :]
        h = _rms(x, gpre_ref[...]).astype(BF16)
        for c in range(0, D_FF, nc):
            gt = _dot(h, wg_ref[:, c:c + nc])
            a_sc[:, c:c + nc] = (gt * jax.nn.sigmoid(gt) * _dot(h, wu_ref[:, c:c + nc])).astype(BF16)
        f = _dot(a_sc[...], wd_ref[...])
        out_ref[0, rows, :] = x + _rms(f, gpost_ref[...])


def _ffn(x, lw):
    bsz, seq, _ = x.shape
    tm = min(TM_FFN, seq)
    tok = pl.BlockSpec((1, tm, D_MODEL), lambda b, i: (b, i, 0))
    consts = [lw["pre_ffn_g"], lw["w_gate"], lw["w_up"], lw["w_down"], lw["post_ffn_g"]]
    return pl.pallas_call(
        _ffn_kernel,
        grid=(bsz, seq // tm),
        in_specs=[tok] + [_const_spec(a) for a in consts],
        out_specs=tok,
        out_shape=jax.ShapeDtypeStruct(x.shape, F32),
        scratch_shapes=[pltpu.VMEM((tm // FFN_SUBBLOCKS, D_FF), BF16)] * FFN_SUBBLOCKS,
        compiler_params=_params(2),
        name="ffn",
    )(x, *consts)


def _rope_tables(seq):
    pos = jnp.arange(seq, dtype=F32)[:, None]

    def tables(dim):
        inv = 1.0 / (ROPE_THETA ** (jnp.arange(0, dim, 2, dtype=F32) / dim))
        ang = pos * inv[None, :]
        return jnp.cos(ang), jnp.sin(ang)

    cb, sb = tables(B_ROPE)
    zb = jnp.zeros((seq, LANES - B_ROPE), F32)
    cc, sc = tables(C_HD)
    return (jnp.concatenate([cb, cb, zb], axis=1), jnp.concatenate([-sb, sb, zb], axis=1),
            jnp.concatenate([cc, cc], axis=1), jnp.concatenate([-sc, sc], axis=1))


def _prep_layer(l, p):
    row = lambda a: a[l][None, :].astype(F32)
    w_in = p["w_in"][l]
    offs = [0]
    for s in IN_SIZES:
        offs.append(offs[-1] + s)
    seg = lambda k: w_in[:, offs[k]:offs[k + 1]]
    w_kr = seg(4)
    half = B_ROPE // 2
    zpad = jnp.zeros((D_MODEL, LANES - B_ROPE), F32)
    w_lat = jnp.concatenate([seg(2), seg(3), w_kr, zpad, w_kr[:, half:], w_kr[:, :half], zpad], axis=1)
    w_uq = p["b_w_uq"][l].reshape(B_Q_RANK, B_HEADS, B_NOPE + B_ROPE)
    q_pe = w_uq[:, :, B_NOPE:]
    zq = jnp.zeros((B_Q_RANK, B_HEADS, LANES - B_ROPE), F32)
    w_uq_main = jnp.concatenate([w_uq[:, :, :B_NOPE], q_pe, zq], axis=2).reshape(B_Q_RANK, B_HEADS * B_QK_PAD)
    w_uq_sw = jnp.concatenate([q_pe[:, :, half:], q_pe[:, :, :half], zq], axis=2).reshape(B_Q_RANK, B_HEADS * LANES)
    w_ukv = p["b_w_ukv"][l].reshape(B_KV_RANK, B_HEADS, B_NOPE + B_VDIM)
    return {
        "pre_mix_g": row(p["pre_mix_g"]),
        "w_uv": jnp.concatenate([seg(0), seg(1)], axis=1).astype(BF16),
        "w_lat": w_lat.astype(BF16),
        "w_c": jnp.concatenate([seg(5), seg(6), seg(7)], axis=1).astype(BF16),
        "w_gates": seg(8).astype(BF16),
        "a_ln_g": row(p["a_ln_g"]),
        "a_ln_b": row(p["a_ln_b"]),
        "b_q_norm_g": row(p["b_q_norm_g"]),
        "b_kv_norm_g": row(p["b_kv_norm_g"]),
        "w_uq": w_uq_main.astype(BF16),
        "w_uq_sw": w_uq_sw.astype(BF16),
        "w_uk": w_ukv[:, :, :B_NOPE].reshape(B_KV_RANK, B_HEADS * B_NOPE).astype(BF16),
        "w_uvv": w_ukv[:, :, B_NOPE:].reshape(B_KV_RANK, B_HEADS * B_VDIM).astype(BF16),
        "a_w_s": p["a_w_s"][l].astype(BF16),
        "a_b_s": jnp.broadcast_to(p["a_b_s"][l][:, :, None], (A_GROUPS, CHUNK, LANES)).astype(F32),
        "c_sink": jnp.broadcast_to((p["c_sink"][l] * LOG2E)[:, None], (C_HEADS, LANES)).astype(F32),
        "w_pa": p["w_pa"][l].astype(BF16),
        "w_pb": p["w_pb"][l].astype(BF16),
        "w_pc": p["w_pc"][l].astype(BF16),
        "w_o": p["w_o"][l].astype(BF16),
        "post_mix_g": row(p["post_mix_g"]),
        "pre_ffn_g": row(p["pre_ffn_g"]),
        "w_gate": p["w_gate"][l].astype(BF16),
        "w_up": p["w_up"][l].astype(BF16),
        "w_down": p["w_down"][l].astype(BF16),
        "post_ffn_g": row(p["post_ffn_g"]),
    }


def _trunk(x, layers):
    rope = _rope_tables(x.shape[1])
    for lw in layers:
        h, u, v, qm, km, vm, cq, ck, cv = _proj(x, lw, rope)
        ob = _mla(qm, km, vm)
        oc = _swa(cq, ck, cv, lw["c_sink"])
        x = _merge(x, h, u, v, ob, oc, lw)
        x = _ffn(x, lw)
    return x


def kernel(x_prompt, x_sample, pre_mix_g, w_in, a_ln_g, a_ln_b, a_w_s, a_b_s, b_q_norm_g, b_w_uq, b_kv_norm_g,
           b_w_ukv, c_sink, w_pa, w_pb, w_pc, w_o, post_mix_g, pre_ffn_g, w_gate, w_up, w_down, post_ffn_g):
    p = dict(pre_mix_g=pre_mix_g, w_in=w_in, a_ln_g=a_ln_g, a_ln_b=a_ln_b, a_w_s=a_w_s, a_b_s=a_b_s,
             b_q_norm_g=b_q_norm_g, b_w_uq=b_w_uq, b_kv_norm_g=b_kv_norm_g, b_w_ukv=b_w_ukv, c_sink=c_sink,
             w_pa=w_pa, w_pb=w_pb, w_pc=w_pc, w_o=w_o, post_mix_g=post_mix_g, pre_ffn_g=pre_ffn_g,
             w_gate=w_gate, w_up=w_up, w_down=w_down, post_ffn_g=post_ffn_g)
    layers = [_prep_layer(l, p) for l in range(w_in.shape[0])]
    return (_trunk(x_prompt, layers), _trunk(x_sample, layers))
```

```python
import functools
import math

import jax
import jax.numpy as jnp
from jax import lax
from jax.experimental import pallas as pl
from jax.experimental.pallas import tpu as pltpu

F32 = jnp.float32
BF16 = jnp.bfloat16

D_MODEL = 1024
EPS = 1e-6
ROPE_THETA = 10000.0
N_BRANCH = 3
CHUNK = 128
A_GROUPS = 8
A_WIDTH = 1024
B_HEADS = 8
B_Q_RANK = 384
B_KV_RANK = 256
B_NOPE = 128
B_ROPE = 64
B_VDIM = 128
C_HEADS = 8
C_KV_HEADS = 2
C_HD = 128
C_WIDTH = C_HEADS * C_HD
C_KV_WIDTH = C_KV_HEADS * C_HD
C_REP = C_HEADS // C_KV_HEADS
WINDOW = 128
D_FF = -(-8 * D_MODEL // (3 * 256)) * 256
IN_SIZES = (A_WIDTH, A_WIDTH, B_Q_RANK, B_KV_RANK, B_ROPE, C_WIDTH, C_KV_WIDTH, C_KV_WIDTH, N_BRANCH * D_MODEL)

LANES = 128
B_QK_PAD = 2 * LANES
LOG2E = math.log2(math.e)
B_QSCALE = (B_NOPE + B_ROPE) ** -0.5 * LOG2E
C_QSCALE = C_HD ** -0.5 * LOG2E

V7X_VMEM_BYTES = 64 * 1024 * 1024
VMEM_LIMIT = V7X_VMEM_BYTES - 8 * 1024 * 1024

TM_PROJ = 512
TM_MERGE = 512
TM_FFN = 1024
FFN_SUBBLOCKS = 2
TQ_MLA = 512
TK_MLA = 2048
MLA_QBLOCKS = 4
SWA_BLOCKS = 16


def _dot(a, b):
    return jnp.dot(a, b, preferred_element_type=F32)


def _dot_nt(a, b):
    return lax.dot_general(a, b, (((1,), (1,)), ((), ())), preferred_element_type=F32)


def _rms(x, g):
    return x * lax.rsqrt(jnp.mean(x * x, axis=-1, keepdims=True) + EPS) * g


def _const_spec(arr):
    nd = arr.ndim
    return pl.BlockSpec(arr.shape, lambda *_: (0,) * nd, pipeline_mode=pl.Buffered(1))


def _params(n_axes, flags=None):
    return pltpu.CompilerParams(dimension_semantics=("arbitrary",) * n_axes, vmem_limit_bytes=VMEM_LIMIT,
                                flags=flags)


def _proj_kernel(x_ref, g_ref, wuv_ref, wlat_ref, wc_ref, lng_ref, lnb_ref, qg_ref, kvg_ref,
                 wuq_ref, wuqs_ref, wuk_ref, wuvv_ref, cb_ref, sb_ref, cc_ref, sc_ref,
                 h_ref, u_ref, v_ref, qm_ref, km_ref, vm_ref, cq_ref, ck_ref, cv_ref):
    h = _rms(x_ref[0], g_ref[...]).astype(BF16)
    h_ref[0] = h

    o1 = B_Q_RANK
    o2 = o1 + B_KV_RANK
    cqn = _rms(_dot(h, wlat_ref[:, :o1]), qg_ref[...]).astype(BF16)
    ckvn = _rms(_dot(h, wlat_ref[:, o1:o2]), kvg_ref[...]).astype(BF16)
    cb = cb_ref[...]
    sb = sb_ref[...]
    kr = _dot(h, wlat_ref[:, o2:])
    kpe = (kr[:, :LANES] * cb + kr[:, LANES:] * sb).astype(BF16)

    nc = 2 * LANES
    v = jnp.concatenate([jax.nn.gelu(_dot(h, wuv_ref[:, A_WIDTH + c:A_WIDTH + c + nc]))
                         for c in range(0, A_WIDTH, nc)], axis=1)
    vc = v - jnp.mean(v, axis=-1, keepdims=True)
    var = jnp.mean(vc * vc, axis=-1, keepdims=True)
    v_ref[0] = (vc * lax.rsqrt(var + EPS) * lng_ref[...] + lnb_ref[...]).astype(BF16)
    for c in range(0, A_WIDTH, nc):
        u_ref[0, :, c:c + nc] = jax.nn.gelu(_dot(h, wuv_ref[:, c:c + nc])).astype(BF16)

    cc = cc_ref[...]
    sc = sc_ref[...]
    for c in range(0, C_WIDTH + C_KV_WIDTH, nc):
        qk = _dot(h, wc_ref[:, c:c + nc])
        for j in range(nc // LANES):
            t = qk[:, j * LANES:(j + 1) * LANES]
            r = t * cc + pltpu.roll(t, C_HD // 2, 1) * sc
            col = c + j * LANES
            if col < C_WIDTH:
                cq_ref[0, :, col:col + LANES] = (r * C_QSCALE).astype(BF16)
            else:
                ck_ref[0, :, col - C_WIDTH:col - C_WIDTH + LANES] = r.astype(BF16)
    cv_ref[0] = _dot(h, wc_ref[:, C_WIDTH + C_KV_WIDTH:]).astype(BF16)

    for hp in range(0, B_HEADS, 2):
        qs2 = _dot(cqn, wuqs_ref[:, hp * LANES:(hp + 2) * LANES])
        for j in range(2):
            c0 = (hp + j) * B_QK_PAD
            qh = _dot(cqn, wuq_ref[:, c0:c0 + B_QK_PAD])
            qs = qs2[:, j * LANES:(j + 1) * LANES]
            qm_ref[0, :, c0:c0 + LANES] = (qh[:, :LANES] * B_QSCALE).astype(BF16)
            qm_ref[0, :, c0 + LANES:c0 + B_QK_PAD] = ((qh[:, LANES:] * cb + qs * sb) * B_QSCALE).astype(BF16)
            km_ref[0, :, c0 + LANES:c0 + B_QK_PAD] = kpe
        kn = _dot(ckvn, wuk_ref[:, hp * LANES:(hp + 2) * LANES])
        km_ref[0, :, hp * B_QK_PAD:hp * B_QK_PAD + LANES] = kn[:, :LANES].astype(BF16)
        km_ref[0, :, (hp + 1) * B_QK_PAD:(hp + 1) * B_QK_PAD + LANES] = kn[:, LANES:].astype(BF16)
        vm_ref[0, :, hp * LANES:(hp + 2) * LANES] = _dot(ckvn, wuvv_ref[:, hp * LANES:(hp + 2) * LANES]).astype(BF16)


def _proj(x, lw, rope):
    bsz, seq, _ = x.shape
    tm = min(TM_PROJ, seq)
    grid = (bsz, seq // tm)
    tok = lambda w: pl.BlockSpec((1, tm, w), lambda b, i: (b, i, 0))
    pos = pl.BlockSpec((tm, LANES), lambda b, i: (i, 0))
    consts = [lw["pre_mix_g"], lw["w_uv"], lw["w_lat"], lw["w_c"], lw["a_ln_g"], lw["a_ln_b"], lw["b_q_norm_g"],
              lw["b_kv_norm_g"], lw["w_uq"], lw["w_uq_sw"], lw["w_uk"], lw["w_uvv"]]
    widths = [D_MODEL, A_WIDTH, A_WIDTH, B_HEADS * B_QK_PAD, B_HEADS * B_QK_PAD, B_HEADS * B_VDIM,
              C_WIDTH, C_KV_WIDTH, C_KV_WIDTH]
    return pl.pallas_call(
        _proj_kernel,
        grid=grid,
        in_specs=[tok(D_MODEL)] + [_const_spec(a) for a in consts] + [pos] * 4,
        out_specs=[tok(w) for w in widths],
        out_shape=[jax.ShapeDtypeStruct((bsz, seq, w), BF16) for w in widths],
        compiler_params=_params(2),
        name="proj",
    )(x, *consts, *rope)


def _mla_kernel(q_ref, k_ref, v_ref, o_ref, *, tq, tk):
    def group(i, carry):
        for qb in range(MLA_QBLOCKS):
            rows = pl.ds(pl.multiple_of((i * MLA_QBLOCKS + qb) * tq, tq), tq)
            o_ref[0, rows, :] = _mla_block(q_ref[0, rows, :], k_ref, v_ref, tk)
        return carry

    lax.fori_loop(0, q_ref.shape[1] // (tq * MLA_QBLOCKS), group, 0)


def _mla_block(q, k_ref, v_ref, tk):
    n = k_ref.shape[1] // tk
    tq = q.shape[0]
    ones = jnp.ones((tk, LANES), BF16)

    def scores(c):
        return _dot_nt(q, k_ref[0, c * tk:(c + 1) * tk, :])

    m = jnp.full((tq, LANES), -jnp.inf, F32)
    acc = jnp.zeros((tq, B_VDIM + LANES), F32)
    s = scores(0)
    for t in range(n):
        s_next = scores(t + 1) if t + 1 < n else None
        chunks = [s[:, c * LANES:(c + 1) * LANES] for c in range(tk // LANES)]
        m_new = jnp.maximum(m, jnp.max(functools.reduce(jnp.maximum, chunks), axis=-1, keepdims=True))
        alpha = jnp.exp2(m - m_new)
        p = jnp.concatenate([jnp.exp2((c - m_new).astype(BF16)) for c in chunks], axis=1)
        v1 = jnp.concatenate([v_ref[0, t * tk:(t + 1) * tk, :], ones], axis=1)
        acc = jnp.concatenate([alpha, alpha], axis=1) * acc + _dot(p, v1)
        m = m_new
        s = s_next
    return (acc[:, :B_VDIM] / acc[:, B_VDIM:]).astype(BF16)


def _mla(qm, km, vm):
    bsz, seq, _ = qm.shape
    tq = min(TQ_MLA, seq // MLA_QBLOCKS)
    tk = min(TK_MLA, seq // 2)
    assert seq % (tq * MLA_QBLOCKS) == 0
    return pl.pallas_call(
        functools.partial(_mla_kernel, tq=tq, tk=tk),
        grid=(bsz, B_HEADS),
        in_specs=[
            pl.BlockSpec((1, seq, B_QK_PAD), lambda b, h: (b, 0, h)),
            pl.BlockSpec((1, seq, B_QK_PAD), lambda b, h: (b, 0, h)),
            pl.BlockSpec((1, seq, B_VDIM), lambda b, h: (b, 0, h)),
        ],
        out_specs=pl.BlockSpec((1, seq, B_VDIM), lambda b, h: (b, 0, h)),
        out_shape=jax.ShapeDtypeStruct((bsz, seq, B_HEADS * B_VDIM), BF16),
        compiler_params=_params(2),
        name="mla",
    )(qm, km, vm)


def _swa_kernel(q_ref, kc_ref, kp_ref, kn_ref, vc_ref, vp_ref, vn_ref, sink_ref, o_ref, *, nblk):
    i = pl.program_id(1)
    last = pl.num_programs(1) - 1
    rows = C_REP * CHUNK
    qi = lax.broadcasted_iota(jnp.int32, (rows, CHUNK), 0) % CHUNK
    kj = lax.broadcasted_iota(jnp.int32, (rows, CHUNK), 1)
    band_prev = kj >= qi
    band_next = kj <= qi
    neg = jnp.float32(-jnp.inf)
    ones = jnp.ones((3 * CHUNK, LANES), BF16)

    def blk(ref_c, ref_e, n, g, edge_lo):
        cols = slice(g * C_HD, (g + 1) * C_HD)
        if edge_lo:
            return ref_e[0, :, cols] if n == 0 else ref_c[0, (n - 1) * CHUNK:n * CHUNK, cols]
        return ref_e[0, :, cols] if n == nblk - 1 else ref_c[0, (n + 1) * CHUNK:(n + 2) * CHUNK, cols]

    for n in range(nblk):
        tok = slice(n * CHUNK, (n + 1) * CHUNK)
        for g in range(C_KV_HEADS):
            cols = slice(g * C_HD, (g + 1) * C_HD)
            q4 = jnp.concatenate([q_ref[0, tok, (g * C_REP + r) * C_HD:(g * C_REP + r + 1) * C_HD]
                                  for r in range(C_REP)], axis=0)
            sink = jnp.concatenate([jnp.broadcast_to(sink_ref[g * C_REP + r:g * C_REP + r + 1, :], (CHUNK, LANES))
                                    for r in range(C_REP)], axis=0)
            k3 = jnp.concatenate([blk(kc_ref, kp_ref, n, g, True), kc_ref[0, tok, cols],
                                  blk(kc_ref, kn_ref, n, g, False)], axis=0)
            v3 = jnp.concatenate([blk(vc_ref, vp_ref, n, g, True), vc_ref[0, tok, cols],
                                  blk(vc_ref, vn_ref, n, g, False)], axis=0)
            s = _dot_nt(q4, k3)
            ok_prev = band_prev if n > 0 else jnp.logical_and(band_prev, i > 0)
            ok_next = band_next if n < nblk - 1 else jnp.logical_and(band_next, i < last)
            s0 = jnp.where(ok_prev, s[:, :CHUNK], neg)
            s1 = s[:, CHUNK:2 * CHUNK]
            s2 = jnp.where(ok_next, s[:, 2 * CHUNK:], neg)
            m = jnp.max(jnp.maximum(jnp.maximum(s0, s1), s2), axis=-1, keepdims=True)
            m = jnp.maximum(m, sink)
            p = jnp.concatenate([jnp.exp2((t - m).astype(BF16)) for t in (s0, s1, s2)], axis=1)
            pv = _dot(p, jnp.concatenate([v3, ones], axis=1))
            o = pv[:, :C_HD] / (pv[:, C_HD:] + jnp.exp2(sink - m))
            for r in range(C_REP):
                hc = (g * C_REP + r) * C_HD
                o_ref[0, tok, hc:hc + C_HD] = o[r * CHUNK:(r + 1) * CHUNK].astype(BF16)


def _swa(cq, ck, cv, sink):
    bsz, seq, _ = cq.shape
    nblk = min(SWA_BLOCKS, seq // CHUNK)
    ts = nblk * CHUNK
    nb = seq // CHUNK
    cur = lambda w: pl.BlockSpec((1, ts, w), lambda b, i: (b, i, 0))
    prev = pl.BlockSpec((1, CHUNK, C_KV_WIDTH), lambda b, i: (b, jnp.maximum(i * nblk - 1, 0), 0))
    nxt = pl.BlockSpec((1, CHUNK, C_KV_WIDTH), lambda b, i: (b, jnp.minimum((i + 1) * nblk, nb - 1), 0))
    return pl.pallas_call(
        functools.partial(_swa_kernel, nblk=nblk),
        grid=(bsz, seq // ts),
        in_specs=[cur(C_WIDTH), cur(C_KV_WIDTH), prev, nxt, cur(C_KV_WIDTH), prev, nxt, _const_spec(sink)],
        out_specs=cur(C_WIDTH),
        out_shape=jax.ShapeDtypeStruct((bsz, seq, C_WIDTH), BF16),
        compiler_params=_params(2),
        name="swa",
    )(cq, ck, ck, ck, cv, cv, cv, sink)


def _merge_kernel(x_ref, h_ref, u_ref, v_ref, ob_ref, oc_ref, wg_ref, ws_ref, bs_ref, wpa_ref, wpb_ref, wpc_ref,
                  wo_ref, g_ref, out_ref, ya_sc, mg_sc):
    tm = x_ref.shape[1]
    for c in range(tm // CHUNK):
        tok = slice(c * CHUNK, (c + 1) * CHUNK)
        for g in range(A_GROUPS):
            cols = slice(g * LANES, (g + 1) * LANES)
            mixed = _dot(ws_ref[g], v_ref[0, tok, cols]) + bs_ref[g]
            ya_sc[tok, cols] = (u_ref[0, tok, cols].astype(F32) * mixed).astype(BF16)
    h = h_ref[0]
    ya = ya_sc[...]
    ob = ob_ref[0]
    oc = oc_ref[0]
    nc = 2 * LANES
    for c in range(0, D_MODEL, nc):
        cols = slice(c, c + nc)
        acc = jax.nn.sigmoid(_dot(h, wg_ref[:, c:c + nc])) * _dot(ya, wpa_ref[:, cols])
        acc += jax.nn.sigmoid(_dot(h, wg_ref[:, D_MODEL + c:D_MODEL + c + nc])) * _dot(ob, wpb_ref[:, cols])
        acc += jax.nn.sigmoid(_dot(h, wg_ref[:, 2 * D_MODEL + c:2 * D_MODEL + c + nc])) * _dot(oc, wpc_ref[:, cols])
        mg_sc[:, cols] = acc.astype(BF16)
    mix = _dot(mg_sc[...], wo_ref[...])
    out_ref[0] = x_ref[0] + _rms(mix, g_ref[...])


def _merge(x, h, u, v, ob, oc, lw):
    bsz, seq, _ = x.shape
    tm = min(TM_MERGE, seq)
    tok = pl.BlockSpec((1, tm, D_MODEL), lambda b, i: (b, i, 0))
    consts = [lw["w_gates"], lw["a_w_s"], lw["a_b_s"], lw["w_pa"], lw["w_pb"], lw["w_pc"], lw["w_o"],
              lw["post_mix_g"]]
    return pl.pallas_call(
        _merge_kernel,
        grid=(bsz, seq // tm),
        in_specs=[tok] * 6 + [_const_spec(a) for a in consts],
        out_specs=tok,
        out_shape=jax.ShapeDtypeStruct(x.shape, F32),
        scratch_shapes=[pltpu.VMEM((tm, A_WIDTH), BF16), pltpu.VMEM((tm, D_MODEL), BF16)],
        compiler_params=_params(2),
        name="merge",
    )(x, h, u, v, ob, oc, *consts)


def _ffn_kernel(x_ref, gpre_ref, wg_ref, wu_ref, wd_ref, gpost_ref, out_ref):
    sub = x_ref.shape[1] // FFN_SUBBLOCKS
    nc = 2 * LANES
    for sb in range(FFN_SUBBLOCKS):
        rows = slice(sb * sub, (sb + 1) * sub)
        x = x_ref[0, rows, :]
        h = _rms(x, gpre_ref[...]).astype(BF16)
        act = []
        for c in range(0, D_FF, nc):
            gt = _dot(h, wg_ref[:, c:c + nc])
            act.append((gt * jax.nn.sigmoid(gt) * _dot(h, wu_ref[:, c:c + nc])).astype(BF16))
        f = _dot(jnp.concatenate(act, axis=1), wd_ref[...])
        out_ref[0, rows, :] = x + _rms(f, gpost_ref[...])


def _ffn(x, lw):
    bsz, seq, _ = x.shape
    tm = min(TM_FFN, seq)
    tok = pl.BlockSpec((1, tm, D_MODEL), lambda b, i: (b, i, 0))
    consts = [lw["pre_ffn_g"], lw["w_gate"], lw["w_up"], lw["w_down"], lw["post_ffn_g"]]
    return pl.pallas_call(
        _ffn_kernel,
        grid=(bsz, seq // tm),
        in_specs=[tok] + [_const_spec(a) for a in consts],
        out_specs=tok,
        out_shape=jax.ShapeDtypeStruct(x.shape, F32),
        compiler_params=_params(2),
        name="ffn",
    )(x, *consts)


def _rope_tables(seq):
    pos = jnp.arange(seq, dtype=F32)[:, None]

    def tables(dim):
        inv = 1.0 / (ROPE_THETA ** (jnp.arange(0, dim, 2, dtype=F32) / dim))
        ang = pos * inv[None, :]
        return jnp.cos(ang), jnp.sin(ang)

    cb, sb = tables(B_ROPE)
    zb = jnp.zeros((seq, LANES - B_ROPE), F32)
    cc, sc = tables(C_HD)
    return (jnp.concatenate([cb, cb, zb], axis=1), jnp.concatenate([-sb, sb, zb], axis=1),
            jnp.concatenate([cc, cc], axis=1), jnp.concatenate([-sc, sc], axis=1))


def _prep_layer(l, p):
    row = lambda a: a[l][None, :].astype(F32)
    w_in = p["w_in"][l]
    offs = [0]
    for s in IN_SIZES:
        offs.append(offs[-1] + s)
    seg = lambda k: w_in[:, offs[k]:offs[k + 1]]
    w_kr = seg(4)
    half = B_ROPE // 2
    zpad = jnp.zeros((D_MODEL, LANES - B_ROPE), F32)
    w_lat = jnp.concatenate([seg(2), seg(3), w_kr, zpad, w_kr[:, half:], w_kr[:, :half], zpad], axis=1)
    w_uq = p["b_w_uq"][l].reshape(B_Q_RANK, B_HEADS, B_NOPE + B_ROPE)
    q_pe = w_uq[:, :, B_NOPE:]
    zq = jnp.zeros((B_Q_RANK, B_HEADS, LANES - B_ROPE), F32)
    w_uq_main = jnp.concatenate([w_uq[:, :, :B_NOPE], q_pe, zq], axis=2).reshape(B_Q_RANK, B_HEADS * B_QK_PAD)
    w_uq_sw = jnp.concatenate([q_pe[:, :, half:], q_pe[:, :, :half], zq], axis=2).reshape(B_Q_RANK, B_HEADS * LANES)
    w_ukv = p["b_w_ukv"][l].reshape(B_KV_RANK, B_HEADS, B_NOPE + B_VDIM)
    return {
        "pre_mix_g": row(p["pre_mix_g"]),
        "w_uv": jnp.concatenate([seg(0), seg(1)], axis=1).astype(BF16),
        "w_lat": w_lat.astype(BF16),
        "w_c": jnp.concatenate([seg(5), seg(6), seg(7)], axis=1).astype(BF16),
        "w_gates": seg(8).astype(BF16),
        "a_ln_g": row(p["a_ln_g"]),
        "a_ln_b": row(p["a_ln_b"]),
        "b_q_norm_g": row(p["b_q_norm_g"]),
        "b_kv_norm_g": row(p["b_kv_norm_g"]),
        "w_uq": w_uq_main.astype(BF16),
        "w_uq_sw": w_uq_sw.astype(BF16),
        "w_uk": w_ukv[:, :, :B_NOPE].reshape(B_KV_RANK, B_HEADS * B_NOPE).astype(BF16),
        "w_uvv": w_ukv[:, :, B_NOPE:].reshape(B_KV_RANK, B_HEADS * B_VDIM).astype(BF16),
        "a_w_s": p["a_w_s"][l].astype(BF16),
        "a_b_s": jnp.broadcast_to(p["a_b_s"][l][:, :, None], (A_GROUPS, CHUNK, LANES)).astype(F32),
        "c_sink": jnp.broadcast_to((p["c_sink"][l] * LOG2E)[:, None], (C_HEADS, LANES)).astype(F32),
        "w_pa": p["w_pa"][l].astype(BF16),
        "w_pb": p["w_pb"][l].astype(BF16),
        "w_pc": p["w_pc"][l].astype(BF16),
        "w_o": p["w_o"][l].astype(BF16),
        "post_mix_g": row(p["post_mix_g"]),
        "pre_ffn_g": row(p["pre_ffn_g"]),
        "w_gate": p["w_gate"][l].astype(BF16),
        "w_up": p["w_up"][l].astype(BF16),
        "w_down": p["w_down"][l].astype(BF16),
        "post_ffn_g": row(p["post_ffn_g"]),
    }


def _trunk(x, layers):
    rope = _rope_tables(x.shape[1])
    for lw in layers:
        h, u, v, qm, km, vm, cq, ck, cv = _proj(x, lw, rope)
        ob = _mla(qm, km, vm)
        oc = _swa(cq, ck, cv, lw["c_sink"])
        x = _merge(x, h, u, v, ob, oc, lw)
        x = _ffn(x, lw)
    return x


def kernel(x_prompt, x_sample, pre_mix_g, w_in, a_ln_g, a_ln_b, a_w_s, a_b_s, b_q_norm_g, b_w_uq, b_kv_norm_g,
           b_w_ukv, c_sink, w_pa, w_pb, w_pc, w_o, post_mix_g, pre_ffn_g, w_gate, w_up, w_down, post_ffn_g):
    p = dict(pre_mix_g=pre_mix_g, w_in=w_in, a_ln_g=a_ln_g, a_ln_b=a_ln_b, a_w_s=a_w_s, a_b_s=a_b_s,
             b_q_norm_g=b_q_norm_g, b_w_uq=b_w_uq, b_kv_norm_g=b_kv_norm_g, b_w_ukv=b_w_ukv, c_sink=c_sink,
             w_pa=w_pa, w_pb=w_pb, w_pc=w_pc, w_o=w_o, post_mix_g=post_mix_g, pre_ffn_g=pre_ffn_g,
             w_gate=w_gate, w_up=w_up, w_down=w_down, post_ffn_g=post_ffn_g)
    layers = [_prep_layer(l, p) for l in range(w_in.shape[0])]
    return (_trunk(x_prompt, layers), _trunk(x_sample, layers))
```

```python
import functools
import math

import jax
import jax.numpy as jnp
from jax import lax
from jax.experimental import pallas as pl
from jax.experimental.pallas import tpu as pltpu

F32 = jnp.float32
BF16 = jnp.bfloat16

D_MODEL = 1024
EPS = 1e-6
ROPE_THETA = 10000.0
N_BRANCH = 3
CHUNK = 128
A_GROUPS = 8
A_WIDTH = 1024
B_HEADS = 8
B_Q_RANK = 384
B_KV_RANK = 256
B_NOPE = 128
B_ROPE = 64
B_VDIM = 128
C_HEADS = 8
C_KV_HEADS = 2
C_HD = 128
C_WIDTH = C_HEADS * C_HD
C_KV_WIDTH = C_KV_HEADS * C_HD
C_REP = C_HEADS // C_KV_HEADS
WINDOW = 128
D_FF = -(-8 * D_MODEL // (3 * 256)) * 256
IN_SIZES = (A_WIDTH, A_WIDTH, B_Q_RANK, B_KV_RANK, B_ROPE, C_WIDTH, C_KV_WIDTH, C_KV_WIDTH, N_BRANCH * D_MODEL)

LANES = 128
B_QK_PAD = 2 * LANES
LOG2E = math.log2(math.e)
B_QSCALE = (B_NOPE + B_ROPE) ** -0.5 * LOG2E
C_QSCALE = C_HD ** -0.5 * LOG2E

V7X_VMEM_BYTES = 64 * 1024 * 1024
VMEM_LIMIT = V7X_VMEM_BYTES * 7 // 8

TM_PROJ = 512
TM_MERGE = 512
TM_FFN = 1024
FFN_SUBBLOCKS = 2
TQ_MLA = 512
TK_MLA = 2048
MLA_QBLOCKS = 4
SWA_BLOCKS = 16


def _dot(a, b):
    return jnp.dot(a, b, preferred_element_type=F32)


def _dot_nt(a, b):
    return lax.dot_general(a, b, (((1,), (1,)), ((), ())), preferred_element_type=F32)


def _rms(x, g):
    return x * lax.rsqrt(jnp.mean(x * x, axis=-1, keepdims=True) + EPS) * g


def _const_spec(arr):
    nd = arr.ndim
    return pl.BlockSpec(arr.shape, lambda *_: (0,) * nd, pipeline_mode=pl.Buffered(1))


def _params(n_axes):
    return pltpu.CompilerParams(dimension_semantics=("arbitrary",) * n_axes, vmem_limit_bytes=VMEM_LIMIT)


def _proj_kernel(x_ref, g_ref, wuv_ref, wlat_ref, wc_ref, lng_ref, lnb_ref, qg_ref, kvg_ref,
                 wuq_ref, wuqs_ref, wuk_ref, wuvv_ref, cb_ref, sb_ref, cc_ref, sc_ref,
                 h_ref, u_ref, v_ref, qm_ref, km_ref, vm_ref, cq_ref, ck_ref, cv_ref):
    h = _rms(x_ref[0], g_ref[...]).astype(BF16)
    h_ref[0] = h

    o1 = B_Q_RANK
    o2 = o1 + B_KV_RANK
    cqn = _rms(_dot(h, wlat_ref[:, :o1]), qg_ref[...]).astype(BF16)
    ckvn = _rms(_dot(h, wlat_ref[:, o1:o2]), kvg_ref[...]).astype(BF16)
    cb = cb_ref[...]
    sb = sb_ref[...]
    kr = _dot(h, wlat_ref[:, o2:])
    kpe = (kr[:, :LANES] * cb + kr[:, LANES:] * sb).astype(BF16)

    nc = 2 * LANES
    v = jnp.concatenate([jax.nn.gelu(_dot(h, wuv_ref[:, A_WIDTH + c:A_WIDTH + c + nc]))
                         for c in range(0, A_WIDTH, nc)], axis=1)
    vc = v - jnp.mean(v, axis=-1, keepdims=True)
    var = jnp.mean(vc * vc, axis=-1, keepdims=True)
    v_ref[0] = (vc * lax.rsqrt(var + EPS) * lng_ref[...] + lnb_ref[...]).astype(BF16)
    for c in range(0, A_WIDTH, nc):
        u_ref[0, :, c:c + nc] = jax.nn.gelu(_dot(h, wuv_ref[:, c:c + nc])).astype(BF16)

    cc = cc_ref[...]
    sc = sc_ref[...]
    for c in range(0, C_WIDTH + C_KV_WIDTH, nc):
        qk = _dot(h, wc_ref[:, c:c + nc])
        for j in range(nc // LANES):
            t = qk[:, j * LANES:(j + 1) * LANES]
            r = t * cc + pltpu.roll(t, C_HD // 2, 1) * sc
            col = c + j * LANES
            if col < C_WIDTH:
                cq_ref[0, :, col:col + LANES] = (r * C_QSCALE).astype(BF16)
            else:
                ck_ref[0, :, col - C_WIDTH:col - C_WIDTH + LANES] = r.astype(BF16)
    cv_ref[0] = _dot(h, wc_ref[:, C_WIDTH + C_KV_WIDTH:]).astype(BF16)

    for hp in range(0, B_HEADS, 2):
        qs2 = _dot(cqn, wuqs_ref[:, hp * LANES:(hp + 2) * LANES])
        for j in range(2):
            c0 = (hp + j) * B_QK_PAD
            qh = _dot(cqn, wuq_ref[:, c0:c0 + B_QK_PAD])
            qs = qs2[:, j * LANES:(j + 1) * LANES]
            qm_ref[0, :, c0:c0 + LANES] = (qh[:, :LANES] * B_QSCALE).astype(BF16)
            qm_ref[0, :, c0 + LANES:c0 + B_QK_PAD] = ((qh[:, LANES:] * cb + qs * sb) * B_QSCALE).astype(BF16)
            km_ref[0, :, c0 + LANES:c0 + B_QK_PAD] = kpe
        kn = _dot(ckvn, wuk_ref[:, hp * LANES:(hp + 2) * LANES])
        km_ref[0, :, hp * B_QK_PAD:hp * B_QK_PAD + LANES] = kn[:, :LANES].astype(BF16)
        km_ref[0, :, (hp + 1) * B_QK_PAD:(hp + 1) * B_QK_PAD + LANES] = kn[:, LANES:].astype(BF16)
        vm_ref[0, :, hp * LANES:(hp + 2) * LANES] = _dot(ckvn, wuvv_ref[:, hp * LANES:(hp + 2) * LANES]).astype(BF16)


def _proj(x, lw, rope):
    bsz, seq, _ = x.shape
    tm = min(TM_PROJ, seq)
    grid = (bsz, seq // tm)
    tok = lambda w: pl.BlockSpec((1, tm, w), lambda b, i: (b, i, 0))
    pos = pl.BlockSpec((tm, LANES), lambda b, i: (i, 0))
    consts = [lw["pre_mix_g"], lw["w_uv"], lw["w_lat"], lw["w_c"], lw["a_ln_g"], lw["a_ln_b"], lw["b_q_norm_g"],
              lw["b_kv_norm_g"], lw["w_uq"], lw["w_uq_sw"], lw["w_uk"], lw["w_uvv"]]
    widths = [D_MODEL, A_WIDTH, A_WIDTH, B_HEADS * B_QK_PAD, B_HEADS * B_QK_PAD, B_HEADS * B_VDIM,
              C_WIDTH, C_KV_WIDTH, C_KV_WIDTH]
    return pl.pallas_call(
        _proj_kernel,
        grid=grid,
        in_specs=[tok(D_MODEL)] + [_const_spec(a) for a in consts] + [pos] * 4,
        out_specs=[tok(w) for w in widths],
        out_shape=[jax.ShapeDtypeStruct((bsz, seq, w), BF16) for w in widths],
        compiler_params=_params(2),
        name="proj",
    )(x, *consts, *rope)


def _mla_kernel(q_ref, k_ref, v_ref, o_ref, *, tq, tk):
    def group(i, carry):
        for qb in range(MLA_QBLOCKS):
            rows = pl.ds(pl.multiple_of((i * MLA_QBLOCKS + qb) * tq, tq), tq)
            o_ref[0, rows, :] = _mla_block(q_ref[0, rows, :], k_ref, v_ref, tk)
        return carry

    lax.fori_loop(0, q_ref.shape[1] // (tq * MLA_QBLOCKS), group, 0)


def _mla_block(q, k_ref, v_ref, tk):
    n = k_ref.shape[1] // tk
    tq = q.shape[0]
    ones = jnp.ones((tk, LANES), BF16)

    def scores(c):
        return _dot_nt(q, k_ref[0, c * tk:(c + 1) * tk, :])

    m = jnp.full((tq, LANES), -jnp.inf, F32)
    acc = jnp.zeros((tq, B_VDIM + LANES), F32)
    s = scores(0)
    for t in range(n):
        s_next = scores(t + 1) if t + 1 < n else None
        chunks = [s[:, c * LANES:(c + 1) * LANES] for c in range(tk // LANES)]
        m_new = jnp.maximum(m, jnp.max(functools.reduce(jnp.maximum, chunks), axis=-1, keepdims=True))
        alpha = jnp.exp2(m - m_new)
        p = jnp.concatenate([jnp.exp2((c - m_new).astype(BF16)) for c in chunks], axis=1)
        v1 = jnp.concatenate([v_ref[0, t * tk:(t + 1) * tk, :], ones], axis=1)
        acc = jnp.concatenate([alpha, alpha], axis=1) * acc + _dot(p, v1)
        m = m_new
        s = s_next
    return (acc[:, :B_VDIM] / acc[:, B_VDIM:]).astype(BF16)


def _mla(qm, km, vm):
    bsz, seq, _ = qm.shape
    tq = min(TQ_MLA, seq // MLA_QBLOCKS)
    tk = min(TK_MLA, seq // 2)
    assert seq % (tq * MLA_QBLOCKS) == 0
    return pl.pallas_call(
        functools.partial(_mla_kernel, tq=tq, tk=tk),
        grid=(bsz, B_HEADS),
        in_specs=[
            pl.BlockSpec((1, seq, B_QK_PAD), lambda b, h: (b, 0, h)),
            pl.BlockSpec((1, seq, B_QK_PAD), lambda b, h: (b, 0, h)),
            pl.BlockSpec((1, seq, B_VDIM), lambda b, h: (b, 0, h)),
        ],
        out_specs=pl.BlockSpec((1, seq, B_VDIM), lambda b, h: (b, 0, h)),
        out_shape=jax.ShapeDtypeStruct((bsz, seq, B_HEADS * B_VDIM), BF16),
        compiler_params=_params(2),
        name="mla",
    )(qm, km, vm)


def _swa_kernel(q_ref, kc_ref, kp_ref, kn_ref, vc_ref, vp_ref, vn_ref, sink_ref, o_ref, *, nblk):
    i = pl.program_id(1)
    last = pl.num_programs(1) - 1
    rows = C_REP * CHUNK
    qi = lax.broadcasted_iota(jnp.int32, (rows, CHUNK), 0) % CHUNK
    kj = lax.broadcasted_iota(jnp.int32, (rows, CHUNK), 1)
    band_prev = kj >= qi
    band_next = kj <= qi
    neg = jnp.float32(-jnp.inf)
    ones = jnp.ones((3 * CHUNK, LANES), BF16)

    def blk(ref_c, ref_e, n, g, edge_lo):
        cols = slice(g * C_HD, (g + 1) * C_HD)
        if edge_lo:
            return ref_e[0, :, cols] if n == 0 else ref_c[0, (n - 1) * CHUNK:n * CHUNK, cols]
        return ref_e[0, :, cols] if n == nblk - 1 else ref_c[0, (n + 1) * CHUNK:(n + 2) * CHUNK, cols]

    for n in range(nblk):
        tok = slice(n * CHUNK, (n + 1) * CHUNK)
        for g in range(C_KV_HEADS):
            cols = slice(g * C_HD, (g + 1) * C_HD)
            q4 = jnp.concatenate([q_ref[0, tok, (g * C_REP + r) * C_HD:(g * C_REP + r + 1) * C_HD]
                                  for r in range(C_REP)], axis=0)
            sink = jnp.concatenate([jnp.broadcast_to(sink_ref[g * C_REP + r:g * C_REP + r + 1, :], (CHUNK, LANES))
                                    for r in range(C_REP)], axis=0)
            k3 = jnp.concatenate([blk(kc_ref, kp_ref, n, g, True), kc_ref[0, tok, cols],
                                  blk(kc_ref, kn_ref, n, g, False)], axis=0)
            v3 = jnp.concatenate([blk(vc_ref, vp_ref, n, g, True), vc_ref[0, tok, cols],
                                  blk(vc_ref, vn_ref, n, g, False)], axis=0)
            s = _dot_nt(q4, k3)
            ok_prev = band_prev if n > 0 else jnp.logical_and(band_prev, i > 0)
            ok_next = band_next if n < nblk - 1 else jnp.logical_and(band_next, i < last)
            s0 = jnp.where(ok_prev, s[:, :CHUNK], neg)
            s1 = s[:, CHUNK:2 * CHUNK]
            s2 = jnp.where(ok_next, s[:, 2 * CHUNK:], neg)
            m = jnp.max(jnp.maximum(jnp.maximum(s0, s1), s2), axis=-1, keepdims=True)
            m = jnp.maximum(m, sink)
            p = jnp.concatenate([jnp.exp2((t - m).astype(BF16)) for t in (s0, s1, s2)], axis=1)
            pv = _dot(p, jnp.concatenate([v3, ones], axis=1))
            o = pv[:, :C_HD] / (pv[:, C_HD:] + jnp.exp2(sink - m))
            for r in range(C_REP):
                hc = (g * C_REP + r) * C_HD
                o_ref[0, tok, hc:hc + C_HD] = o[r * CHUNK:(r + 1) * CHUNK].astype(BF16)


def _swa(cq, ck, cv, sink):
    bsz, seq, _ = cq.shape
    nblk = min(SWA_BLOCKS, seq // CHUNK)
    ts = nblk * CHUNK
    nb = seq // CHUNK
    cur = lambda w: pl.BlockSpec((1, ts, w), lambda b, i: (b, i, 0))
    prev = pl.BlockSpec((1, CHUNK, C_KV_WIDTH), lambda b, i: (b, jnp.maximum(i * nblk - 1, 0), 0))
    nxt = pl.BlockSpec((1, CHUNK, C_KV_WIDTH), lambda b, i: (b, jnp.minimum((i + 1) * nblk, nb - 1), 0))
    return pl.pallas_call(
        functools.partial(_swa_kernel, nblk=nblk),
        grid=(bsz, seq // ts),
        in_specs=[cur(C_WIDTH), cur(C_KV_WIDTH), prev, nxt, cur(C_KV_WIDTH), prev, nxt, _const_spec(sink)],
        out_specs=cur(C_WIDTH),
        out_shape=jax.ShapeDtypeStruct((bsz, seq, C_WIDTH), BF16),
        compiler_params=_params(2),
        name="swa",
    )(cq, ck, ck, ck, cv, cv, cv, sink)


def _merge_kernel(x_ref, h_ref, u_ref, v_ref, ob_ref, oc_ref, wg_ref, ws_ref, bs_ref, wpa_ref, wpb_ref, wpc_ref,
                  wo_ref, g_ref, out_ref, ya_sc, mg_sc):
    tm = x_ref.shape[1]
    for c in range(tm // CHUNK):
        tok = slice(c * CHUNK, (c + 1) * CHUNK)
        for g in range(A_GROUPS):
            cols = slice(g * LANES, (g + 1) * LANES)
            mixed = _dot(ws_ref[g], v_ref[0, tok, cols]) + bs_ref[g]
            ya_sc[tok, cols] = (u_ref[0, tok, cols].astype(F32) * mixed).astype(BF16)
    h = h_ref[0]
    ya = ya_sc[...]
    ob = ob_ref[0]
    oc = oc_ref[0]
    nc = 2 * LANES
    for c in range(0, D_MODEL, nc):
        cols = slice(c, c + nc)
        acc = jax.nn.sigmoid(_dot(h, wg_ref[:, c:c + nc])) * _dot(ya, wpa_ref[:, cols])
        acc += jax.nn.sigmoid(_dot(h, wg_ref[:, D_MODEL + c:D_MODEL + c + nc])) * _dot(ob, wpb_ref[:, cols])
        acc += jax.nn.sigmoid(_dot(h, wg_ref[:, 2 * D_MODEL + c:2 * D_MODEL + c + nc])) * _dot(oc, wpc_ref[:, cols])
        mg_sc[:, cols] = acc.astype(BF16)
    mix = _dot(mg_sc[...], wo_ref[...])
    out_ref[0] = x_ref[0] + _rms(mix, g_ref[...])


def _merge(x, h, u, v, ob, oc, lw):
    bsz, seq, _ = x.shape
    tm = min(TM_MERGE, seq)
    tok = pl.BlockSpec((1, tm, D_MODEL), lambda b, i: (b, i, 0))
    consts = [lw["w_gates"], lw["a_w_s"], lw["a_b_s"], lw["w_pa"], lw["w_pb"], lw["w_pc"], lw["w_o"],
              lw["post_mix_g"]]
    return pl.pallas_call(
        _merge_kernel,
        grid=(bsz, seq // tm),
        in_specs=[tok] * 6 + [_const_spec(a) for a in consts],
        out_specs=tok,
        out_shape=jax.ShapeDtypeStruct(x.shape, F32),
        scratch_shapes=[pltpu.VMEM((tm, A_WIDTH), BF16), pltpu.VMEM((tm, D_MODEL), BF16)],
        compiler_params=_params(2),
        name="merge",
    )(x, h, u, v, ob, oc, *consts)


def _ffn_kernel(x_ref, gpre_ref, wg_ref, wu_ref, wd_ref, gpost_ref, out_ref):
    sub = x_ref.shape[1] // FFN_SUBBLOCKS
    nc = 2 * LANES
    for sb in range(FFN_SUBBLOCKS):
        rows = slice(sb * sub, (sb + 1) * sub)
        x = x_ref[0, rows, :]
        h = _rms(x, gpre_ref[...]).astype(BF16)
        act = []
        for c in range(0, D_FF, nc):
            gt = _dot(h, wg_ref[:, c:c + nc])
            act.append((gt * jax.nn.sigmoid(gt) * _dot(h, wu_ref[:, c:c + nc])).astype(BF16))
        f = _dot(jnp.concatenate(act, axis=1), wd_ref[...])
        out_ref[0, rows, :] = x + _rms(f, gpost_ref[...])


def _ffn(x, lw):
    bsz, seq, _ = x.shape
    tm = min(TM_FFN, seq)
    tok = pl.BlockSpec((1, tm, D_MODEL), lambda b, i: (b, i, 0))
    consts = [lw["pre_ffn_g"], lw["w_gate"], lw["w_up"], lw["w_down"], lw["post_ffn_g"]]
    return pl.pallas_call(
        _ffn_kernel,
        grid=(bsz, seq // tm),
        in_specs=[tok] + [_const_spec(a) for a in consts],
        out_specs=tok,
        out_shape=jax.ShapeDtypeStruct(x.shape, F32),
        compiler_params=_params(2),
        name="ffn",
    )(x, *consts)


def _rope_tables(seq):
    pos = jnp.arange(seq, dtype=F32)[:, None]

    def tables(dim):
        inv = 1.0 / (ROPE_THETA ** (jnp.arange(0, dim, 2, dtype=F32) / dim))
        ang = pos * inv[None, :]
        return jnp.cos(ang), jnp.sin(ang)

    cb, sb = tables(B_ROPE)
    zb = jnp.zeros((seq, LANES - B_ROPE), F32)
    cc, sc = tables(C_HD)
    return (jnp.concatenate([cb, cb, zb], axis=1), jnp.concatenate([-sb, sb, zb], axis=1),
            jnp.concatenate([cc, cc], axis=1), jnp.concatenate([-sc, sc], axis=1))


def _prep_layer(l, p):
    row = lambda a: a[l][None, :].astype(F32)
    w_in = p["w_in"][l]
    offs = [0]
    for s in IN_SIZES:
        offs.append(offs[-1] + s)
    seg = lambda k: w_in[:, offs[k]:offs[k + 1]]
    w_kr = seg(4)
    half = B_ROPE // 2
    zpad = jnp.zeros((D_MODEL, LANES - B_ROPE), F32)
    w_lat = jnp.concatenate([seg(2), seg(3), w_kr, zpad, w_kr[:, half:], w_kr[:, :half], zpad], axis=1)
    w_uq = p["b_w_uq"][l].reshape(B_Q_RANK, B_HEADS, B_NOPE + B_ROPE)
    q_pe = w_uq[:, :, B_NOPE:]
    zq = jnp.zeros((B_Q_RANK, B_HEADS, LANES - B_ROPE), F32)
    w_uq_main = jnp.concatenate([w_uq[:, :, :B_NOPE], q_pe, zq], axis=2).reshape(B_Q_RANK, B_HEADS * B_QK_PAD)
    w_uq_sw = jnp.concatenate([q_pe[:, :, half:], q_pe[:, :, :half], zq], axis=2).reshape(B_Q_RANK, B_HEADS * LANES)
    w_ukv = p["b_w_ukv"][l].reshape(B_KV_RANK, B_HEADS, B_NOPE + B_VDIM)
    return {
        "pre_mix_g": row(p["pre_mix_g"]),
        "w_uv": jnp.concatenate([seg(0), seg(1)], axis=1).astype(BF16),
        "w_lat": w_lat.astype(BF16),
        "w_c": jnp.concatenate([seg(5), seg(6), seg(7)], axis=1).astype(BF16),
        "w_gates": seg(8).astype(BF16),
        "a_ln_g": row(p["a_ln_g"]),
        "a_ln_b": row(p["a_ln_b"]),
        "b_q_norm_g": row(p["b_q_norm_g"]),
        "b_kv_norm_g": row(p["b_kv_norm_g"]),
        "w_uq": w_uq_main.astype(BF16),
        "w_uq_sw": w_uq_sw.astype(BF16),
        "w_uk": w_ukv[:, :, :B_NOPE].reshape(B_KV_RANK, B_HEADS * B_NOPE).astype(BF16),
        "w_uvv": w_ukv[:, :, B_NOPE:].reshape(B_KV_RANK, B_HEADS * B_VDIM).astype(BF16),
        "a_w_s": p["a_w_s"][l].astype(BF16),
        "a_b_s": jnp.broadcast_to(p["a_b_s"][l][:, :, None], (A_GROUPS, CHUNK, LANES)).astype(F32),
        "c_sink": jnp.broadcast_to((p["c_sink"][l] * LOG2E)[:, None], (C_HEADS, LANES)).astype(F32),
        "w_pa": p["w_pa"][l].astype(BF16),
        "w_pb": p["w_pb"][l].astype(BF16),
        "w_pc": p["w_pc"][l].astype(BF16),
        "w_o": p["w_o"][l].astype(BF16),
        "post_mix_g": row(p["post_mix_g"]),
        "pre_ffn_g": row(p["pre_ffn_g"]),
        "w_gate": p["w_gate"][l].astype(BF16),
        "w_up": p["w_up"][l].astype(BF16),
        "w_down": p["w_down"][l].astype(BF16),
        "post_ffn_g": row(p["post_ffn_g"]),
    }


def _trunk(x, layers):
    rope = _rope_tables(x.shape[1])
    for lw in layers:
        h, u, v, qm, km, vm, cq, ck, cv = _proj(x, lw, rope)
        ob = _mla(qm, km, vm)
        oc = _swa(cq, ck, cv, lw["c_sink"])
        x = _merge(x, h, u, v, ob, oc, lw)
        x = _ffn(x, lw)
    return x


def kernel(x_prompt, x_sample, pre_mix_g, w_in, a_ln_g, a_ln_b, a_w_s, a_b_s, b_q_norm_g, b_w_uq, b_kv_norm_g,
           b_w_ukv, c_sink, w_pa, w_pb, w_pc, w_o, post_mix_g, pre_ffn_g, w_gate, w_up, w_down, post_ffn_g):
    p = dict(pre_mix_g=pre_mix_g, w_in=w_in, a_ln_g=a_ln_g, a_ln_b=a_ln_b, a_w_s=a_w_s, a_b_s=a_b_s,
             b_q_norm_g=b_q_norm_g, b_w_uq=b_w_uq, b_kv_norm_g=b_kv_norm_g, b_w_ukv=b_w_ukv, c_sink=c_sink,
             w_pa=w_pa, w_pb=w_pb, w_pc=w_pc, w_o=w_o, post_mix_g=post_mix_g, pre_ffn_g=pre_ffn_g,
             w_gate=w_gate, w_up=w_up, w_down=w_down, post_ffn_g=post_ffn_g)
    layers = [_prep_layer(l, p) for l in range(w_in.shape[0])]
    return (_trunk(x_prompt, layers), _trunk(x_sample, layers))
```

```python
import functools
import math

import jax
import jax.numpy as jnp
from jax import lax
from jax.experimental import pallas as pl
from jax.experimental.pallas import tpu as pltpu

F32 = jnp.float32
BF16 = jnp.bfloat16

D_MODEL = 1024
EPS = 1e-6
ROPE_THETA = 10000.0
N_BRANCH = 3
CHUNK = 128
A_GROUPS = 8
A_WIDTH = 1024
B_HEADS = 8
B_Q_RANK = 384
B_KV_RANK = 256
B_NOPE = 128
B_ROPE = 64
B_VDIM = 128
C_HEADS = 8
C_KV_HEADS = 2
C_HD = 128
C_WIDTH = C_HEADS * C_HD
C_KV_WIDTH = C_KV_HEADS * C_HD
C_REP = C_HEADS // C_KV_HEADS
WINDOW = 128
D_FF = -(-8 * D_MODEL // (3 * 256)) * 256
IN_SIZES = (A_WIDTH, A_WIDTH, B_Q_RANK, B_KV_RANK, B_ROPE, C_WIDTH, C_KV_WIDTH, C_KV_WIDTH, N_BRANCH * D_MODEL)

LANES = 128
B_QK_PAD = 2 * LANES
LOG2E = math.log2(math.e)
B_QSCALE = (B_NOPE + B_ROPE) ** -0.5 * LOG2E
C_QSCALE = C_HD ** -0.5 * LOG2E

V7X_VMEM_BYTES = 64 * 1024 * 1024
VMEM_LIMIT = V7X_VMEM_BYTES * 7 // 8

TM_PROJ = 512
PROJ_SUBBLOCKS = 2
TM_MERGE = 512
TM_FFN = 1024
FFN_SUBBLOCKS = 2
TQ_MLA = 512
TK_MLA = 2048
MLA_QBLOCKS = 4
SWA_BLOCKS = 16


def _dot(a, b):
    return jnp.dot(a, b, preferred_element_type=F32)


def _dot_nt(a, b):
    return lax.dot_general(a, b, (((1,), (1,)), ((), ())), preferred_element_type=F32)


def _rms(x, g):
    return x * lax.rsqrt(jnp.mean(x * x, axis=-1, keepdims=True) + EPS) * g


def _const_spec(arr):
    nd = arr.ndim
    return pl.BlockSpec(arr.shape, lambda *_: (0,) * nd, pipeline_mode=pl.Buffered(1))


def _params(n_axes):
    return pltpu.CompilerParams(dimension_semantics=("arbitrary",) * n_axes, vmem_limit_bytes=VMEM_LIMIT)


def _proj_kernel(x_ref, *refs):
    consts, pos, outs = refs[:12], refs[12:16], refs[16:]
    sub = x_ref.shape[1] // PROJ_SUBBLOCKS
    for sb in range(PROJ_SUBBLOCKS):
        rows = slice(sb * sub, (sb + 1) * sub)
        _proj_rows(x_ref.at[:, rows, :], *consts, *[r.at[rows, :] for r in pos], *[r.at[:, rows, :] for r in outs])


def _proj_rows(x_ref, g_ref, wuv_ref, wlat_ref, wc_ref, lng_ref, lnb_ref, qg_ref, kvg_ref,
               wuq_ref, wuqs_ref, wuk_ref, wuvv_ref, cb_ref, sb_ref, cc_ref, sc_ref,
               h_ref, u_ref, v_ref, qm_ref, km_ref, vm_ref, cq_ref, ck_ref, cv_ref):
    h = _rms(x_ref[0], g_ref[...]).astype(BF16)
    h_ref[0] = h

    o1 = B_Q_RANK
    o2 = o1 + B_KV_RANK
    cqn = _rms(_dot(h, wlat_ref[:, :o1]), qg_ref[...]).astype(BF16)
    ckvn = _rms(_dot(h, wlat_ref[:, o1:o2]), kvg_ref[...]).astype(BF16)
    cb = cb_ref[...]
    sb = sb_ref[...]
    kr = _dot(h, wlat_ref[:, o2:])
    kpe = (kr[:, :LANES] * cb + kr[:, LANES:] * sb).astype(BF16)

    nc = 2 * LANES
    v = jnp.concatenate([jax.nn.gelu(_dot(h, wuv_ref[:, A_WIDTH + c:A_WIDTH + c + nc]))
                         for c in range(0, A_WIDTH, nc)], axis=1)
    vc = v - jnp.mean(v, axis=-1, keepdims=True)
    var = jnp.mean(vc * vc, axis=-1, keepdims=True)
    v_ref[0] = (vc * lax.rsqrt(var + EPS) * lng_ref[...] + lnb_ref[...]).astype(BF16)
    for c in range(0, A_WIDTH, nc):
        u_ref[0, :, c:c + nc] = jax.nn.gelu(_dot(h, wuv_ref[:, c:c + nc])).astype(BF16)

    cc = cc_ref[...]
    sc = sc_ref[...]
    for c in range(0, C_WIDTH + C_KV_WIDTH, nc):
        qk = _dot(h, wc_ref[:, c:c + nc])
        for j in range(nc // LANES):
            t = qk[:, j * LANES:(j + 1) * LANES]
            r = t * cc + pltpu.roll(t, C_HD // 2, 1) * sc
            col = c + j * LANES
            if col < C_WIDTH:
                cq_ref[0, :, col:col + LANES] = (r * C_QSCALE).astype(BF16)
            else:
                ck_ref[0, :, col - C_WIDTH:col - C_WIDTH + LANES] = r.astype(BF16)
    cv_ref[0] = _dot(h, wc_ref[:, C_WIDTH + C_KV_WIDTH:]).astype(BF16)

    for hp in range(0, B_HEADS, 2):
        qs2 = _dot(cqn, wuqs_ref[:, hp * LANES:(hp + 2) * LANES])
        for j in range(2):
            c0 = (hp + j) * B_QK_PAD
            qh = _dot(cqn, wuq_ref[:, c0:c0 + B_QK_PAD])
            qs = qs2[:, j * LANES:(j + 1) * LANES]
            qm_ref[0, :, c0:c0 + LANES] = (qh[:, :LANES] * B_QSCALE).astype(BF16)
            qm_ref[0, :, c0 + LANES:c0 + B_QK_PAD] = ((qh[:, LANES:] * cb + qs * sb) * B_QSCALE).astype(BF16)
            km_ref[0, :, c0 + LANES:c0 + B_QK_PAD] = kpe
        kn = _dot(ckvn, wuk_ref[:, hp * LANES:(hp + 2) * LANES])
        km_ref[0, :, hp * B_QK_PAD:hp * B_QK_PAD + LANES] = kn[:, :LANES].astype(BF16)
        km_ref[0, :, (hp + 1) * B_QK_PAD:(hp + 1) * B_QK_PAD + LANES] = kn[:, LANES:].astype(BF16)
        vm_ref[0, :, hp * LANES:(hp + 2) * LANES] = _dot(ckvn, wuvv_ref[:, hp * LANES:(hp + 2) * LANES]).astype(BF16)


def _proj(x, lw, rope):
    bsz, seq, _ = x.shape
    tm = min(TM_PROJ, seq)
    grid = (bsz, seq // tm)
    tok = lambda w: pl.BlockSpec((1, tm, w), lambda b, i: (b, i, 0))
    pos = pl.BlockSpec((tm, LANES), lambda b, i: (i, 0))
    consts = [lw["pre_mix_g"], lw["w_uv"], lw["w_lat"], lw["w_c"], lw["a_ln_g"], lw["a_ln_b"], lw["b_q_norm_g"],
              lw["b_kv_norm_g"], lw["w_uq"], lw["w_uq_sw"], lw["w_uk"], lw["w_uvv"]]
    widths = [D_MODEL, A_WIDTH, A_WIDTH, B_HEADS * B_QK_PAD, B_HEADS * B_QK_PAD, B_HEADS * B_VDIM,
              C_WIDTH, C_KV_WIDTH, C_KV_WIDTH]
    return pl.pallas_call(
        _proj_kernel,
        grid=grid,
        in_specs=[tok(D_MODEL)] + [_const_spec(a) for a in consts] + [pos] * 4,
        out_specs=[tok(w) for w in widths],
        out_shape=[jax.ShapeDtypeStruct((bsz, seq, w), BF16) for w in widths],
        compiler_params=_params(2),
        name="proj",
    )(x, *consts, *rope)


def _mla_kernel(q_ref, k_ref, v_ref, o_ref, *, tq, tk):
    def group(i, carry):
        for qb in range(MLA_QBLOCKS):
            rows = pl.ds(pl.multiple_of((i * MLA_QBLOCKS + qb) * tq, tq), tq)
            o_ref[0, rows, :] = _mla_block(q_ref[0, rows, :], k_ref, v_ref, tk)
        return carry

    lax.fori_loop(0, q_ref.shape[1] // (tq * MLA_QBLOCKS), group, 0)


def _mla_block(q, k_ref, v_ref, tk):
    n = k_ref.shape[1] // tk
    tq = q.shape[0]
    ones = jnp.ones((tk, LANES), BF16)

    def scores(c):
        return _dot_nt(q, k_ref[0, c * tk:(c + 1) * tk, :])

    m = jnp.full((tq, LANES), -jnp.inf, F32)
    acc = jnp.zeros((tq, B_VDIM + LANES), F32)
    s = scores(0)
    for t in range(n):
        s_next = scores(t + 1) if t + 1 < n else None
        chunks = [s[:, c * LANES:(c + 1) * LANES] for c in range(tk // LANES)]
        m_new = jnp.maximum(m, jnp.max(functools.reduce(jnp.maximum, chunks), axis=-1, keepdims=True))
        alpha = jnp.exp2(m - m_new)
        p = jnp.concatenate([jnp.exp2((c - m_new).astype(BF16)) for c in chunks], axis=1)
        v1 = jnp.concatenate([v_ref[0, t * tk:(t + 1) * tk, :], ones], axis=1)
        acc = jnp.concatenate([alpha, alpha], axis=1) * acc + _dot(p, v1)
        m = m_new
        s = s_next
    return (acc[:, :B_VDIM] / acc[:, B_VDIM:]).astype(BF16)


def _mla(qm, km, vm):
    bsz, seq, _ = qm.shape
    tq = min(TQ_MLA, seq // MLA_QBLOCKS)
    tk = min(TK_MLA, seq // 2)
    assert seq % (tq * MLA_QBLOCKS) == 0
    return pl.pallas_call(
        functools.partial(_mla_kernel, tq=tq, tk=tk),
        grid=(bsz, B_HEADS),
        in_specs=[
            pl.BlockSpec((1, seq, B_QK_PAD), lambda b, h: (b, 0, h)),
            pl.BlockSpec((1, seq, B_QK_PAD), lambda b, h: (b, 0, h)),
            pl.BlockSpec((1, seq, B_VDIM), lambda b, h: (b, 0, h)),
        ],
        out_specs=pl.BlockSpec((1, seq, B_VDIM), lambda b, h: (b, 0, h)),
        out_shape=jax.ShapeDtypeStruct((bsz, seq, B_HEADS * B_VDIM), BF16),
        compiler_params=_params(2),
        name="mla",
    )(qm, km, vm)


def _swa_kernel(q_ref, kc_ref, kp_ref, kn_ref, vc_ref, vp_ref, vn_ref, sink_ref, o_ref, *, nblk):
    i = pl.program_id(1)
    last = pl.num_programs(1) - 1
    rows = C_REP * CHUNK
    qi = lax.broadcasted_iota(jnp.int32, (rows, CHUNK), 0) % CHUNK
    kj = lax.broadcasted_iota(jnp.int32, (rows, CHUNK), 1)
    band_prev = kj >= qi
    band_next = kj <= qi
    neg = jnp.float32(-jnp.inf)
    ones = jnp.ones((3 * CHUNK, LANES), BF16)

    def blk(ref_c, ref_e, n, g, edge_lo):
        cols = slice(g * C_HD, (g + 1) * C_HD)
        if edge_lo:
            return ref_e[0, :, cols] if n == 0 else ref_c[0, (n - 1) * CHUNK:n * CHUNK, cols]
        return ref_e[0, :, cols] if n == nblk - 1 else ref_c[0, (n + 1) * CHUNK:(n + 2) * CHUNK, cols]

    for n in range(nblk):
        tok = slice(n * CHUNK, (n + 1) * CHUNK)
        for g in range(C_KV_HEADS):
            cols = slice(g * C_HD, (g + 1) * C_HD)
            q4 = jnp.concatenate([q_ref[0, tok, (g * C_REP + r) * C_HD:(g * C_REP + r + 1) * C_HD]
                                  for r in range(C_REP)], axis=0)
            sink = jnp.concatenate([jnp.broadcast_to(sink_ref[g * C_REP + r:g * C_REP + r + 1, :], (CHUNK, LANES))
                                    for r in range(C_REP)], axis=0)
            k3 = jnp.concatenate([blk(kc_ref, kp_ref, n, g, True), kc_ref[0, tok, cols],
                                  blk(kc_ref, kn_ref, n, g, False)], axis=0)
            v3 = jnp.concatenate([blk(vc_ref, vp_ref, n, g, True), vc_ref[0, tok, cols],
                                  blk(vc_ref, vn_ref, n, g, False)], axis=0)
            s = _dot_nt(q4, k3)
            ok_prev = band_prev if n > 0 else jnp.logical_and(band_prev, i > 0)
            ok_next = band_next if n < nblk - 1 else jnp.logical_and(band_next, i < last)
            s0 = jnp.where(ok_prev, s[:, :CHUNK], neg)
            s1 = s[:, CHUNK:2 * CHUNK]
            s2 = jnp.where(ok_next, s[:, 2 * CHUNK:], neg)
            m = jnp.max(jnp.maximum(jnp.maximum(s0, s1), s2), axis=-1, keepdims=True)
            m = jnp.maximum(m, sink)
            p = jnp.concatenate([jnp.exp2((t - m).astype(BF16)) for t in (s0, s1, s2)], axis=1)
            pv = _dot(p, jnp.concatenate([v3, ones], axis=1))
            o = pv[:, :C_HD] / (pv[:, C_HD:] + jnp.exp2(sink - m))
            for r in range(C_REP):
                hc = (g * C_REP + r) * C_HD
                o_ref[0, tok, hc:hc + C_HD] = o[r * CHUNK:(r + 1) * CHUNK].astype(BF16)


def _swa(cq, ck, cv, sink):
    bsz, seq, _ = cq.shape
    nblk = min(SWA_BLOCKS, seq // CHUNK)
    ts = nblk * CHUNK
    nb = seq // CHUNK
    cur = lambda w: pl.BlockSpec((1, ts, w), lambda b, i: (b, i, 0))
    prev = pl.BlockSpec((1, CHUNK, C_KV_WIDTH), lambda b, i: (b, jnp.maximum(i * nblk - 1, 0), 0))
    nxt = pl.BlockSpec((1, CHUNK, C_KV_WIDTH), lambda b, i: (b, jnp.minimum((i + 1) * nblk, nb - 1), 0))
    return pl.pallas_call(
        functools.partial(_swa_kernel, nblk=nblk),
        grid=(bsz, seq // ts),
        in_specs=[cur(C_WIDTH), cur(C_KV_WIDTH), prev, nxt, cur(C_KV_WIDTH), prev, nxt, _const_spec(sink)],
        out_specs=cur(C_WIDTH),
        out_shape=jax.ShapeDtypeStruct((bsz, seq, C_WIDTH), BF16),
        compiler_params=_params(2),
        name="swa",
    )(cq, ck, ck, ck, cv, cv, cv, sink)


def _merge_kernel(x_ref, h_ref, u_ref, v_ref, ob_ref, oc_ref, wg_ref, ws_ref, bs_ref, wpa_ref, wpb_ref, wpc_ref,
                  wo_ref, g_ref, out_ref, ya_sc, mg_sc):
    tm = x_ref.shape[1]
    for c in range(tm // CHUNK):
        tok = slice(c * CHUNK, (c + 1) * CHUNK)
        for g in range(A_GROUPS):
            cols = slice(g * LANES, (g + 1) * LANES)
            mixed = _dot(ws_ref[g], v_ref[0, tok, cols]) + bs_ref[g]
            ya_sc[tok, cols] = (u_ref[0, tok, cols].astype(F32) * mixed).astype(BF16)
    h = h_ref[0]
    ya = ya_sc[...]
    ob = ob_ref[0]
    oc = oc_ref[0]
    nc = 2 * LANES
    for c in range(0, D_MODEL, nc):
        cols = slice(c, c + nc)
        acc = jax.nn.sigmoid(_dot(h, wg_ref[:, c:c + nc])) * _dot(ya, wpa_ref[:, cols])
        acc += jax.nn.sigmoid(_dot(h, wg_ref[:, D_MODEL + c:D_MODEL + c + nc])) * _dot(ob, wpb_ref[:, cols])
        acc += jax.nn.sigmoid(_dot(h, wg_ref[:, 2 * D_MODEL + c:2 * D_MODEL + c + nc])) * _dot(oc, wpc_ref[:, cols])
        mg_sc[:, cols] = acc.astype(BF16)
    mix = _dot(mg_sc[...], wo_ref[...])
    out_ref[0] = x_ref[0] + _rms(mix, g_ref[...])


def _merge(x, h, u, v, ob, oc, lw):
    bsz, seq, _ = x.shape
    tm = min(TM_MERGE, seq)
    tok = pl.BlockSpec((1, tm, D_MODEL), lambda b, i: (b, i, 0))
    consts = [lw["w_gates"], lw["a_w_s"], lw["a_b_s"], lw["w_pa"], lw["w_pb"], lw["w_pc"], lw["w_o"],
              lw["post_mix_g"]]
    return pl.pallas_call(
        _merge_kernel,
        grid=(bsz, seq // tm),
        in_specs=[tok] * 6 + [_const_spec(a) for a in consts],
        out_specs=tok,
        out_shape=jax.ShapeDtypeStruct(x.shape, F32),
        scratch_shapes=[pltpu.VMEM((tm, A_WIDTH), BF16), pltpu.VMEM((tm, D_MODEL), BF16)],
        compiler_params=_params(2),
        name="merge",
    )(x, h, u, v, ob, oc, *consts)


def _ffn_kernel(x_ref, gpre_ref, wg_ref, wu_ref, wd_ref, gpost_ref, out_ref):
    sub = x_ref.shape[1] // FFN_SUBBLOCKS
    nc = 2 * LANES
    for sb in range(FFN_SUBBLOCKS):
        rows = slice(sb * sub, (sb + 1) * sub)
        x = x_ref[0, rows, :]
        h = _rms(x, gpre_ref[...]).astype(BF16)
        act = []
        for c in range(0, D_FF, nc):
            gt = _dot(h, wg_ref[:, c:c + nc])
            act.append((gt * jax.nn.sigmoid(gt) * _dot(h, wu_ref[:, c:c + nc])).astype(BF16))
        f = _dot(jnp.concatenate(act, axis=1), wd_ref[...])
        out_ref[0, rows, :] = x + _rms(f, gpost_ref[...])


def _ffn(x, lw):
    bsz, seq, _ = x.shape
    tm = min(TM_FFN, seq)
    tok = pl.BlockSpec((1, tm, D_MODEL), lambda b, i: (b, i, 0))
    consts = [lw["pre_ffn_g"], lw["w_gate"], lw["w_up"], lw["w_down"], lw["post_ffn_g"]]
    return pl.pallas_call(
        _ffn_kernel,
        grid=(bsz, seq // tm),
        in_specs=[tok] + [_const_spec(a) for a in consts],
        out_specs=tok,
        out_shape=jax.ShapeDtypeStruct(x.shape, F32),
        compiler_params=_params(2),
        name="ffn",
    )(x, *consts)


def _rope_tables(seq):
    pos = jnp.arange(seq, dtype=F32)[:, None]

    def tables(dim):
        inv = 1.0 / (ROPE_THETA ** (jnp.arange(0, dim, 2, dtype=F32) / dim))
        ang = pos * inv[None, :]
        return jnp.cos(ang), jnp.sin(ang)

    cb, sb = tables(B_ROPE)
    zb = jnp.zeros((seq, LANES - B_ROPE), F32)
    cc, sc = tables(C_HD)
    return (jnp.concatenate([cb, cb, zb], axis=1), jnp.concatenate([-sb, sb, zb], axis=1),
            jnp.concatenate([cc, cc], axis=1), jnp.concatenate([-sc, sc], axis=1))


def _prep_layer(l, p):
    row = lambda a: a[l][None, :].astype(F32)
    w_in = p["w_in"][l]
    offs = [0]
    for s in IN_SIZES:
        offs.append(offs[-1] + s)
    seg = lambda k: w_in[:, offs[k]:offs[k + 1]]
    w_kr = seg(4)
    half = B_ROPE // 2
    zpad = jnp.zeros((D_MODEL, LANES - B_ROPE), F32)
    w_lat = jnp.concatenate([seg(2), seg(3), w_kr, zpad, w_kr[:, half:], w_kr[:, :half], zpad], axis=1)
    w_uq = p["b_w_uq"][l].reshape(B_Q_RANK, B_HEADS, B_NOPE + B_ROPE)
    q_pe = w_uq[:, :, B_NOPE:]
    zq = jnp.zeros((B_Q_RANK, B_HEADS, LANES - B_ROPE), F32)
    w_uq_main = jnp.concatenate([w_uq[:, :, :B_NOPE], q_pe, zq], axis=2).reshape(B_Q_RANK, B_HEADS * B_QK_PAD)
    w_uq_sw = jnp.concatenate([q_pe[:, :, half:], q_pe[:, :, :half], zq], axis=2).reshape(B_Q_RANK, B_HEADS * LANES)
    w_ukv = p["b_w_ukv"][l].reshape(B_KV_RANK, B_HEADS, B_NOPE + B_VDIM)
    return {
        "pre_mix_g": row(p["pre_mix_g"]),
        "w_uv": jnp.concatenate([seg(0), seg(1)], axis=1).astype(BF16),
        "w_lat": w_lat.astype(BF16),
        "w_c": jnp.concatenate([seg(5), seg(6), seg(7)], axis=1).astype(BF16),
        "w_gates": seg(8).astype(BF16),
        "a_ln_g": row(p["a_ln_g"]),
        "a_ln_b": row(p["a_ln_b"]),
        "b_q_norm_g": row(p["b_q_norm_g"]),
        "b_kv_norm_g": row(p["b_kv_norm_g"]),
        "w_uq": w_uq_main.astype(BF16),
        "w_uq_sw": w_uq_sw.astype(BF16),
        "w_uk": w_ukv[:, :, :B_NOPE].reshape(B_KV_RANK, B_HEADS * B_NOPE).astype(BF16),
        "w_uvv": w_ukv[:, :, B_NOPE:].reshape(B_KV_RANK, B_HEADS * B_VDIM).astype(BF16),
        "a_w_s": p["a_w_s"][l].astype(BF16),
        "a_b_s": jnp.broadcast_to(p["a_b_s"][l][:, :, None], (A_GROUPS, CHUNK, LANES)).astype(F32),
        "c_sink": jnp.broadcast_to((p["c_sink"][l] * LOG2E)[:, None], (C_HEADS, LANES)).astype(F32),
        "w_pa": p["w_pa"][l].astype(BF16),
        "w_pb": p["w_pb"][l].astype(BF16),
        "w_pc": p["w_pc"][l].astype(BF16),
        "w_o": p["w_o"][l].astype(BF16),
        "post_mix_g": row(p["post_mix_g"]),
        "pre_ffn_g": row(p["pre_ffn_g"]),
        "w_gate": p["w_gate"][l].astype(BF16),
        "w_up": p["w_up"][l].astype(BF16),
        "w_down": p["w_down"][l].astype(BF16),
        "post_ffn_g": row(p["post_ffn_g"]),
    }


def _trunk(x, layers):
    rope = _rope_tables(x.shape[1])
    for lw in layers:
        h, u, v, qm, km, vm, cq, ck, cv = _proj(x, lw, rope)
        ob = _mla(qm, km, vm)
        oc = _swa(cq, ck, cv, lw["c_sink"])
        x = _merge(x, h, u, v, ob, oc, lw)
        x = _ffn(x, lw)
    return x


def kernel(x_prompt, x_sample, pre_mix_g, w_in, a_ln_g, a_ln_b, a_w_s, a_b_s, b_q_norm_g, b_w_uq, b_kv_norm_g,
           b_w_ukv, c_sink, w_pa, w_pb, w_pc, w_o, post_mix_g, pre_ffn_g, w_gate, w_up, w_down, post_ffn_g):
    p = dict(pre_mix_g=pre_mix_g, w_in=w_in, a_ln_g=a_ln_g, a_ln_b=a_ln_b, a_w_s=a_w_s, a_b_s=a_b_s,
             b_q_norm_g=b_q_norm_g, b_w_uq=b_w_uq, b_kv_norm_g=b_kv_norm_g, b_w_ukv=b_w_ukv, c_sink=c_sink,
             w_pa=w_pa, w_pb=w_pb, w_pc=w_pc, w_o=w_o, post_mix_g=post_mix_g, pre_ffn_g=pre_ffn_g,
             w_gate=w_gate, w_up=w_up, w_down=w_down, post_ffn_g=post_ffn_g)
    layers = [_prep_layer(l, p) for l in range(w_in.shape[0])]
    return (_trunk(x_prompt, layers), _trunk(x_sample, layers))
```
